```python
import jax, jax.numpy as jnp
from jax import lax
import numpy as np


D_MODEL = 2048
BATCH = 4
SEQ = 4096
DEPTH = 1
DEC_BATCH = 1
DEC_SEQ = 16384
PAST_LEN = 128

GRID_W = 64
HEAD_DIM = 128
N_Q_HEADS = 8
N_KV_HEADS = 2
ATTN_WIDTH = N_Q_HEADS * HEAD_DIM
KV_WIDTH = N_KV_HEADS * HEAD_DIM
ROPE_THETA = 10000.0
Q_BLOCK = 128
GLA_HEADS = 4
GLA_DK = 128
GLA_DV = 256
GLA_K_WIDTH = GLA_HEADS * GLA_DK
GLA_V_WIDTH = GLA_HEADS * GLA_DV
GLA_GATE_RANK = 16
GLA_GATE_NORMALIZER = 16.0
GLA_CHUNK = 64
N_BRANCHES = 2
D_FF = 4 * D_MODEL
NORM_EPS = 1e-6

IN_SPLITS = (ATTN_WIDTH, KV_WIDTH, KV_WIDTH,
             GLA_K_WIDTH, GLA_K_WIDTH, GLA_V_WIDTH, GLA_V_WIDTH,
             GLA_GATE_RANK, GLA_GATE_RANK,
             N_BRANCHES * D_MODEL)
D_IN_PROJ = sum(IN_SPLITS)

kernel_name = 'hybrid_gqa_gla_gated_encoder'


def rms_norm(x, gain):
    xf = x.astype(jnp.float32)
    y = xf * lax.rsqrt(jnp.mean(xf * xf, axis=-1, keepdims=True) + NORM_EPS)
    return (y * gain.astype(jnp.float32)).astype(x.dtype)


def axial_rope_tables(T):
    n_rows = T // GRID_W
    rows = jnp.repeat(jnp.arange(n_rows, dtype=jnp.float32), GRID_W)
    cols = jnp.tile(jnp.arange(GRID_W, dtype=jnp.float32), n_rows)
    sec = HEAD_DIM // 2
    inv_freq = ROPE_THETA ** (-jnp.arange(0, sec, 2, dtype=jnp.float32) / sec)
    ang_r = rows[:, None] * inv_freq[None, :]
    ang_c = cols[:, None] * inv_freq[None, :]
    return jnp.cos(ang_r), jnp.sin(ang_r), jnp.cos(ang_c), jnp.sin(ang_c)


def rotate_section(x, cos, sin):
    h = x.shape[-1] // 2
    x1, x2 = x[..., :h], x[..., h:]
    c = cos[None, :, None, :]
    s = sin[None, :, None, :]
    return jnp.concatenate([x1 * c - x2 * s, x1 * s + x2 * c], axis=-1)


def apply_axial_rope(x, tables):
    cos_r, sin_r, cos_c, sin_c = tables
    sec = HEAD_DIM // 2
    xf = x.astype(jnp.float32)
    out = jnp.concatenate([rotate_section(xf[..., :sec], cos_r, sin_r),
                           rotate_section(xf[..., sec:], cos_c, sin_c)], axis=-1)
    return out


def gqa_attention(q, k, v, q_gain, k_gain):
    B, T = q.shape[0], q.shape[1]
    dt = v.dtype
    q = q.reshape(B, T, N_Q_HEADS, HEAD_DIM)
    k = k.reshape(B, T, N_KV_HEADS, HEAD_DIM)
    v = v.reshape(B, T, N_KV_HEADS, HEAD_DIM)
    tables = axial_rope_tables(T)
    q = (apply_axial_rope(rms_norm(q, q_gain), tables) * (HEAD_DIM ** -0.5)).astype(dt)
    k = apply_axial_rope(rms_norm(k, k_gain), tables).astype(dt)
    G = N_Q_HEADS // N_KV_HEADS
    nblk = T // Q_BLOCK
    qb = q.reshape(B, nblk, Q_BLOCK, N_KV_HEADS, G, HEAD_DIM).transpose(1, 0, 3, 4, 2, 5)
    kt = k.transpose(0, 2, 1, 3)
    vt = v.transpose(0, 2, 1, 3)

    def block(qi):
        s = jnp.einsum('bkgqd,bktd->bkgqt', qi, kt, preferred_element_type=jnp.float32)
        p = jax.nn.softmax(s, axis=-1)
        return jnp.einsum('bkgqt,bktd->bkgqd', p.astype(dt), vt)

    o = lax.map(block, qb)
    return o.transpose(1, 0, 4, 2, 3, 5).reshape(B, T, ATTN_WIDTH)


def gla_scan(q, k, v, log_a, strict):
    B, T, H, dk = q.shape
    dv = v.shape[-1]
    C = GLA_CHUNK
    nc = T // C

    def chunks(t):
        return t.reshape(B, nc, C, H, t.shape[-1]).transpose(1, 0, 3, 2, 4)

    q, k, v, log_a = chunks(q), chunks(k), chunks(v), chunks(log_a)
    b = jnp.cumsum(log_a, axis=3)
    b_last = b[:, :, :, -1:, :]
    qe = q * jnp.exp(b)
    ke = k * jnp.exp(-b)
    kd = k * jnp.exp(b_last - b)
    mask = jnp.tril(jnp.ones((C, C), dtype=bool), -1 if strict else 0)
    A = jnp.where(mask, jnp.einsum('nbhid,nbhjd->nbhij', qe, ke), 0.0)
    o_intra = jnp.einsum('nbhij,nbhjv->nbhiv', A, v)
    decay = jnp.exp(b_last)

    def step(S, inp):
        qe_n, kd_n, v_n, decay_n = inp
        o = jnp.einsum('bhid,bhdv->bhiv', qe_n, S)
        S = decay_n[..., 0, :, None] * S + jnp.einsum('bhjd,bhjv->bhdv', kd_n, v_n)
        return S, o

    S0 = jnp.zeros((B, H, dk, dv), jnp.float32)
    _, o_inter = lax.scan(step, S0, (qe, kd, v, decay))
    o = o_intra + o_inter
    return o.transpose(1, 0, 3, 2, 4).reshape(B, T, H, dv)


def gla_branch(q, k, v, g, lr_f, lr_b, w_up_f, b_f, w_up_b, b_b, norm_gain):
    B, T = q.shape[0], q.shape[1]
    dt = v.dtype
    f32 = jnp.float32
    qf = q.astype(f32).reshape(B, T, GLA_HEADS, GLA_DK) * (GLA_DK ** -0.5)
    kf = k.astype(f32).reshape(B, T, GLA_HEADS, GLA_DK)
    vf = v.astype(f32).reshape(B, T, GLA_HEADS, GLA_DV)
    la_f = (jax.nn.log_sigmoid((lr_f @ w_up_f + b_f).astype(f32)) / GLA_GATE_NORMALIZER
            ).reshape(B, T, GLA_HEADS, GLA_DK)
    la_b = (jax.nn.log_sigmoid((lr_b @ w_up_b + b_b).astype(f32)) / GLA_GATE_NORMALIZER
            ).reshape(B, T, GLA_HEADS, GLA_DK)
    o_f = gla_scan(qf, kf, vf, la_f, False)
    flip = lambda t: jnp.flip(t, axis=1)
    o_b = flip(gla_scan(flip(qf), flip(kf), flip(vf), flip(la_b), True))
    o = rms_norm(o_f + o_b, norm_gain)
    o = o.reshape(B, T, GLA_V_WIDTH) * jax.nn.silu(g.astype(f32))
    return o.astype(dt)


def encoder_layer(x, norm_mix, w_in, q_norm, k_norm, w_gate_up_fwd, b_gate_fwd,
                  w_gate_up_bwd, b_gate_bwd, gla_norm, w_attn_proj, w_gla_proj, b_merge,
                  w_out, norm_mlp, w_up, w_down):
    xn = rms_norm(x, norm_mix)
    proj = xn @ w_in
    offsets = np.cumsum(np.array(IN_SPLITS))[:-1].tolist()
    (q_a, k_a, v_a, q_g, k_g, v_g, g_g, lr_f, lr_b, gate_logits) = jnp.split(proj, offsets, axis=-1)
    a = gqa_attention(q_a, k_a, v_a, q_norm, k_norm) @ w_attn_proj
    b = gla_branch(q_g, k_g, v_g, g_g, lr_f, lr_b, w_gate_up_fwd, b_gate_fwd,
                   w_gate_up_bwd, b_gate_bwd, gla_norm) @ w_gla_proj
    gates = jax.nn.sigmoid((gate_logits + b_merge).astype(jnp.float32))
    g_a, g_b = gates[..., :D_MODEL], gates[..., D_MODEL:]
    mixed = (g_a * a.astype(jnp.float32) + g_b * b.astype(jnp.float32)).astype(x.dtype)
    h = x + mixed @ w_out
    hn = rms_norm(h, norm_mlp)
    u = jnp.square(jax.nn.relu(hn @ w_up))
    return h + u @ w_down


def trunk(x, norm_mix, w_in, q_norm, k_norm, w_gate_up_fwd, b_gate_fwd, w_gate_up_bwd,
          b_gate_bwd, gla_norm, w_attn_proj, w_gla_proj, b_merge, w_out, norm_mlp, w_up,
          w_down, norm_final):
    for l in range(DEPTH):
        x = encoder_layer(x, norm_mix[l], w_in[l], q_norm[l], k_norm[l], w_gate_up_fwd[l],
                          b_gate_fwd[l], w_gate_up_bwd[l], b_gate_bwd[l], gla_norm[l],
                          w_attn_proj[l], w_gla_proj[l], b_merge[l], w_out[l], norm_mlp[l],
                          w_up[l], w_down[l])
    return rms_norm(x, norm_final)


def setup_inputs(seed: int = 0) -> dict:
    key = jax.random.key(seed)
    ks = jax.random.split(key, 20)
    f32 = jnp.float32
    nrm = lambda k, shape, scale: jax.random.normal(k, shape, f32) * scale
    gain = lambda k, shape: 1.0 + 0.02 * jax.random.normal(k, shape, f32)
    L = DEPTH
    return {
        'x_prompt': jax.random.normal(ks[0], (BATCH, SEQ, D_MODEL), f32),
        'x_sample': jax.random.normal(ks[1], (DEC_BATCH, DEC_SEQ, D_MODEL), f32),
        'norm_mix': gain(ks[2], (L, D_MODEL)),
        'w_in': nrm(ks[3], (L, D_MODEL, D_IN_PROJ), D_MODEL ** -0.5),
        'q_norm': gain(ks[4], (L, HEAD_DIM)),
        'k_norm': gain(ks[5], (L, HEAD_DIM)),
        'w_gate_up_fwd': nrm(ks[6], (L, GLA_GATE_RANK, GLA_K_WIDTH), GLA_GATE_RANK ** -0.5),
        'b_gate_fwd': nrm(ks[7], (L, GLA_K_WIDTH), 0.02),
        'w_gate_up_bwd': nrm(ks[8], (L, GLA_GATE_RANK, GLA_K_WIDTH), GLA_GATE_RANK ** -0.5),
        'b_gate_bwd': nrm(ks[9], (L, GLA_K_WIDTH), 0.02),
        'gla_norm': gain(ks[10], (L, GLA_DV)),
        'w_attn_proj': nrm(ks[11], (L, ATTN_WIDTH, D_MODEL), ATTN_WIDTH ** -0.5),
        'w_gla_proj': nrm(ks[12], (L, GLA_V_WIDTH, D_MODEL), GLA_V_WIDTH ** -0.5),
        'b_merge': nrm(ks[13], (L, N_BRANCHES * D_MODEL), 0.02),
        'w_out': nrm(ks[14], (L, D_MODEL, D_MODEL), D_MODEL ** -0.5),
        'norm_mlp': gain(ks[15], (L, D_MODEL)),
        'w_up': nrm(ks[16], (L, D_MODEL, D_FF), D_MODEL ** -0.5),
        'w_down': nrm(ks[17], (L, D_FF, D_MODEL), D_FF ** -0.5),
        'norm_final': gain(ks[18], (D_MODEL,)),
    }


def reference(x_prompt, x_sample, norm_mix, w_in, q_norm, k_norm, w_gate_up_fwd, b_gate_fwd,
              w_gate_up_bwd, b_gate_bwd, gla_norm, w_attn_proj, w_gla_proj, b_merge, w_out,
              norm_mlp, w_up, w_down, norm_final):
    y_prompt = trunk(x_prompt, norm_mix, w_in, q_norm, k_norm, w_gate_up_fwd, b_gate_fwd,
                     w_gate_up_bwd, b_gate_bwd, gla_norm, w_attn_proj, w_gla_proj, b_merge,
                     w_out, norm_mlp, w_up, w_down, norm_final)
    y_sample = trunk(x_sample, norm_mix, w_in, q_norm, k_norm, w_gate_up_fwd, b_gate_fwd,
                     w_gate_up_bwd, b_gate_bwd, gla_norm, w_attn_proj, w_gla_proj, b_merge,
                     w_out, norm_mlp, w_up, w_down, norm_final)
    return (y_prompt, y_sample)
```

```python
import functools

import jax
import jax.numpy as jnp
from jax import lax
from jax.experimental import pallas as pl
from jax.experimental.pallas import tpu as pltpu

F32 = jnp.float32
BF16 = jnp.bfloat16

D_MODEL = 2048
DEPTH = 1
GRID_W = 64
HEAD_DIM = 128
N_Q_HEADS = 8
N_KV_HEADS = 2
Q_GROUP = N_Q_HEADS // N_KV_HEADS
ATTN_WIDTH = N_Q_HEADS * HEAD_DIM
KV_WIDTH = N_KV_HEADS * HEAD_DIM
ROPE_THETA = 10000.0
GLA_HEADS = 4
GLA_DK = 128
GLA_DV = 256
GLA_K_WIDTH = GLA_HEADS * GLA_DK
GLA_V_WIDTH = GLA_HEADS * GLA_DV
GLA_GATE_RANK = 16
GLA_GATE_NORMALIZER = 16.0
GLA_CHUNK = 64
N_BRANCHES = 2
D_FF = 4 * D_MODEL
NORM_EPS = 1e-6

LANES = 128
VMEM_LIMIT = 48 * 1024 * 1024

COL_GATE = 0
COL_QA = COL_GATE + N_BRANCHES * D_MODEL
COL_VG = COL_QA + ATTN_WIDTH
COL_GG = COL_VG + GLA_V_WIDTH
COL_KVA = COL_GG + GLA_V_WIDTH
COL_QG = COL_KVA + 2 * KV_WIDTH
COL_KG = COL_QG + GLA_K_WIDTH
COL_LR = COL_KG + GLA_K_WIDTH
PROJ_TN = 1280
PROJ_W = 7 * PROJ_TN

GLA_TB = 256
ATT_TQ = 256
ATT_TK = 512


def _params(sem):
    return pltpu.CompilerParams(dimension_semantics=sem, vmem_limit_bytes=VMEM_LIMIT)


def _sigmoid(x):
    return 1.0 / (1.0 + jnp.exp(-x))


def _rms(x, gain):
    ms = jnp.mean(x * x, axis=-1, keepdims=True)
    return x * lax.rsqrt(ms + NORM_EPS) * gain


def _inproj_kernel(x_ref, g_ref, w_ref, o_ref, xn_ref):
    @pl.when(pl.program_id(1) == 0)
    def _():
        xn_ref[...] = _rms(x_ref[...], g_ref[...]).astype(BF16)

    o_ref[...] = jnp.dot(xn_ref[...], w_ref[...], preferred_element_type=F32)


def _inproj(x, gain, w):
    n = x.shape[0]
    tm = 1024
    return pl.pallas_call(
        _inproj_kernel,
        out_shape=jax.ShapeDtypeStruct((n, PROJ_W), F32),
        grid=(n // tm, PROJ_W // PROJ_TN),
        in_specs=[
            pl.BlockSpec((tm, D_MODEL), lambda i, j: (i, 0)),
            pl.BlockSpec((1, D_MODEL), lambda i, j: (0, 0)),
            pl.BlockSpec((D_MODEL, PROJ_TN), lambda i, j: (0, j)),
        ],
        out_specs=pl.BlockSpec((tm, PROJ_TN), lambda i, j: (i, j)),
        scratch_shapes=[pltpu.VMEM((tm, D_MODEL), BF16)],
        compiler_params=_params(("parallel", "arbitrary")),
    )(x, gain, w)


def _prep_kernel(q_ref, kv_ref, cos_ref, sa_ref, sb_ref, qg_ref, kg_ref, qo_ref, kt_ref, vo_ref):
    cos = cos_ref[...]
    sa = sa_ref[...]
    sb = sb_ref[...]

    def norm_rope(x, gain):
        y = _rms(x, gain)
        return y * cos + pltpu.roll(y, 96, 1) * sa + pltpu.roll(y, 32, 1) * sb

    qg = qg_ref[...]
    kg = kg_ref[...]
    for h in range(N_Q_HEADS):
        sl = slice(h * HEAD_DIM, (h + 1) * HEAD_DIM)
        qo_ref[:, sl] = (norm_rope(q_ref[:, sl], qg) * (HEAD_DIM ** -0.5)).astype(BF16)
    for h in range(N_KV_HEADS):
        sl = slice(h * HEAD_DIM, (h + 1) * HEAD_DIM)
        kt_ref[0, h, 0] = norm_rope(kv_ref[:, sl], kg).T.astype(BF16)
    vo_ref[...] = kv_ref[:, KV_WIDTH:].astype(BF16)


def _prep(proj, tables, q_gain, k_gain, b, t):
    n = b * t
    tm = ATT_TK
    nt = t // tm
    cos, sa, sb = tables
    tab_spec = pl.BlockSpec((tm, HEAD_DIM), lambda i: (i % nt, 0))
    gain_spec = pl.BlockSpec((1, HEAD_DIM), lambda i: (0, 0))
    return pl.pallas_call(
        _prep_kernel,
        out_shape=(
            jax.ShapeDtypeStruct((n, ATTN_WIDTH), BF16),
            jax.ShapeDtypeStruct((b, N_KV_HEADS, nt, HEAD_DIM, tm), BF16),
            jax.ShapeDtypeStruct((n, KV_WIDTH), BF16),
        ),
        grid=(n // tm,),
        in_specs=[
            pl.BlockSpec((tm, ATTN_WIDTH), lambda i: (i, COL_QA // ATTN_WIDTH)),
            pl.BlockSpec((tm, 2 * KV_WIDTH), lambda i: (i, COL_KVA // (2 * KV_WIDTH))),
            tab_spec, tab_spec, tab_spec, gain_spec, gain_spec,
        ],
        out_specs=(
            pl.BlockSpec((tm, ATTN_WIDTH), lambda i: (i, 0)),
            pl.BlockSpec((1, N_KV_HEADS, 1, HEAD_DIM, tm), lambda i: (i // nt, 0, i % nt, 0, 0)),
            pl.BlockSpec((tm, KV_WIDTH), lambda i: (i, 0)),
        ),
        compiler_params=_params(("parallel",)),
    )(proj, proj, cos, sa, sb, q_gain, k_gain)


def _rope_tables(t):
    n_rows = t // GRID_W
    rows = jnp.repeat(jnp.arange(n_rows, dtype=F32), GRID_W)
    cols = jnp.tile(jnp.arange(GRID_W, dtype=F32), n_rows)
    sec = HEAD_DIM // 2
    inv_freq = ROPE_THETA ** (-jnp.arange(0, sec, 2, dtype=F32) / sec)
    ang_r = rows[:, None] * inv_freq[None, :]
    ang_c = cols[:, None] * inv_freq[None, :]
    zero = jnp.zeros_like(ang_r)
    cos = jnp.concatenate([jnp.cos(ang_r)] * 2 + [jnp.cos(ang_c)] * 2, axis=1)
    sin_a = jnp.concatenate([-jnp.sin(ang_r), zero, -jnp.sin(ang_c), zero], axis=1)
    sin_b = jnp.concatenate([zero, jnp.sin(ang_r), zero, jnp.sin(ang_c)], axis=1)
    return cos, sin_a, sin_b


def _attn_kernel(q_ref, kt_ref, v_ref, o_ref, *, nk):
    tq = q_ref.shape[0]
    tk = kt_ref.shape[-1]
    q = jnp.concatenate(
        [q_ref[:, g * HEAD_DIM:(g + 1) * HEAD_DIM] for g in range(Q_GROUP)], axis=0)

    def body(j, carry):
        m, l, acc = carry
        s = jnp.dot(q, kt_ref[0, 0, j], preferred_element_type=F32)
        m_new = jnp.maximum(m, jnp.max(s, axis=-1, keepdims=True))
        alpha = jnp.exp(m - m_new)
        p = jnp.exp(s - m_new)
        l = alpha * l + jnp.sum(p, axis=-1, keepdims=True)
        v = v_ref[pl.ds(pl.multiple_of(j * tk, tk), tk), :]
        acc = alpha * acc + jnp.dot(p.astype(BF16), v, preferred_element_type=F32)
        return m_new, l, acc

    rows = Q_GROUP * tq
    init = (jnp.full((rows, 1), -jnp.inf, F32), jnp.zeros((rows, 1), F32),
            jnp.zeros((rows, HEAD_DIM), F32))
    _, l, acc = lax.fori_loop(0, nk, body, init)
    out = acc * (1.0 / l)
    for g in range(Q_GROUP):
        o_ref[:, g * HEAD_DIM:(g + 1) * HEAD_DIM] = out[g * tq:(g + 1) * tq].astype(BF16)


def _attention(q, kt, v, b, t):
    n = b * t
    tq = ATT_TQ
    nq = t // tq
    nk = t // ATT_TK
    gw = Q_GROUP * HEAD_DIM
    return pl.pallas_call(
        functools.partial(_attn_kernel, nk=nk),
        out_shape=jax.ShapeDtypeStruct((n, ATTN_WIDTH), BF16),
        grid=(b, N_KV_HEADS, nq),
        in_specs=[
            pl.BlockSpec((tq, gw), lambda bi, h, i: (bi * nq + i, h)),
            pl.BlockSpec((1, 1, nk, HEAD_DIM, ATT_TK), lambda bi, h, i: (bi, h, 0, 0, 0)),
            pl.BlockSpec((t, HEAD_DIM), lambda bi, h, i: (bi, h)),
        ],
        out_specs=pl.BlockSpec((tq, gw), lambda bi, h, i: (bi * nq + i, h)),
        compiler_params=_params(("parallel", "parallel", "arbitrary")),
    )(q, kt, v)


def _gla_dir(q_ref, k_ref, v_ref, lr_ref, w_ref, b_ref, tri_ref, mask_ref, o_ref, s_ref, cs_ref,
             reverse):
    c_sz = GLA_CHUNK
    n_chunks = GLA_TB // c_sz
    z = jnp.dot(lr_ref[...].astype(BF16), w_ref[...], preferred_element_type=F32) + b_ref[...]
    la = (jnp.minimum(z, 0.0) - jnp.log1p(jnp.exp(-jnp.abs(z)))) * (1.0 / GLA_GATE_NORMALIZER)
    hi = la.astype(BF16)
    lo = (la - hi.astype(F32)).astype(BF16)
    tri = tri_ref[...]
    cs = (jnp.dot(tri, hi, preferred_element_type=F32)
          + jnp.dot(tri, lo, preferred_element_type=F32))
    for h in range(GLA_HEADS):
        cs_ref[h] = cs[:, h * GLA_DK:(h + 1) * GLA_DK]
    half = c_sz // 2
    edge_start = 0 if reverse else half - 1
    edge_row = [2 * c if reverse else 2 * c + 1 for c in range(n_chunks)]

    qe = (q_ref[...] * (GLA_DK ** -0.5)) * jnp.exp(cs)
    ke = k_ref[...] * jnp.exp(-cs)
    qe_b = qe.astype(BF16)
    ke_b = ke.astype(BF16)
    v_b = v_ref[...].astype(BF16)
    mask = mask_ref[...] > 0.5
    chunk_of_lane = lax.broadcasted_iota(jnp.int32, (GLA_DK, GLA_TB), 1) // c_sz
    order = range(n_chunks - 1, -1, -1) if reverse else range(n_chunks)

    for h in range(GLA_HEADS):
        ks = slice(h * GLA_DK, (h + 1) * GLA_DK)
        vs = slice(h * GLA_DV, (h + 1) * GLA_DV)
        a = lax.dot_general(qe_b[:, ks], ke_b[:, ks], (((1,), (1,)), ((), ())),
                            preferred_element_type=F32)
        a = jnp.where(mask, a, 0.0).astype(BF16)
        o_intra = jnp.dot(a, v_b[:, vs], preferred_element_type=F32)
        edge = jnp.exp(cs_ref[h, pl.ds(edge_start, 2 * n_chunks, stride=half), :])
        decay_t = jnp.tile(edge, (GLA_DK // (2 * n_chunks), 1)).T
        kd = ke[:, ks] * jnp.concatenate(
            [jnp.broadcast_to(edge[r:r + 1], (c_sz, GLA_DK)) for r in edge_row], axis=0)
        kd_t = kd.T
        s = s_ref[h]
        for c in order:
            rs = slice(c * c_sz, (c + 1) * c_sz)
            r = edge_row[c]
            o_inter = jnp.dot(qe_b[rs, ks], s.astype(BF16), preferred_element_type=F32)
            o_ref[rs, vs] = o_intra[rs] + o_inter
            kd_c = jnp.where(chunk_of_lane == c, kd_t, 0.0).astype(BF16)
            s = decay_t[:, r:r + 1] * s + jnp.dot(kd_c, v_b[:, vs], preferred_element_type=F32)
        s_ref[h] = s


def _gla_kernel(qf_ref, kf_ref, vf_ref, lrf_ref, qb_ref, kb_ref, vb_ref, lrb_ref,
                wf_ref, bf_ref, wb_ref, bb_ref, trif_ref, trib_ref, maskf_ref, maskb_ref,
                of_ref, ob_ref, s_ref, cs_ref):
    @pl.when(pl.program_id(1) == 0)
    def _():
        s_ref[...] = jnp.zeros_like(s_ref)

    _gla_dir(qf_ref, kf_ref, vf_ref, lrf_ref, wf_ref, bf_ref, trif_ref, maskf_ref, of_ref,
             s_ref.at[0], cs_ref, reverse=False)
    _gla_dir(qb_ref, kb_ref, vb_ref, lrb_ref, wb_ref, bb_ref, trib_ref, maskb_ref, ob_ref,
             s_ref.at[1], cs_ref, reverse=True)


def _gla_consts():
    i = jnp.arange(GLA_TB)[:, None]
    j = jnp.arange(GLA_TB)[None, :]
    same = (i // GLA_CHUNK) == (j // GLA_CHUNK)
    tri_f = (same & (j <= i)).astype(F32)
    tri_b = (same & (j >= i)).astype(F32)
    mask_b = (same & (j > i)).astype(F32)
    return tri_f, tri_b, mask_b


def _gla(proj, w_f, b_f, w_b, b_b, b, t):
    n = b * t
    tb = GLA_TB
    nt = t // tb
    tri_f, tri_b, mask_b = _gla_consts()

    def fwd(width, col):
        return pl.BlockSpec((tb, width), lambda bi, i: (bi * nt + i, col // width))

    def bwd(width, col):
        return pl.BlockSpec((tb, width), lambda bi, i: (bi * nt + nt - 1 - i, col // width))

    def const(shape):
        return pl.BlockSpec(shape, lambda bi, i: (0, 0))

    pieces = ((GLA_K_WIDTH, COL_QG), (GLA_K_WIDTH, COL_KG), (GLA_V_WIDTH, COL_VG), (LANES, COL_LR))
    out_sd = jax.ShapeDtypeStruct((n, GLA_V_WIDTH), F32)
    return pl.pallas_call(
        _gla_kernel,
        out_shape=(out_sd, out_sd),
        grid=(b, nt),
        in_specs=[fwd(*p) for p in pieces] + [bwd(*p) for p in pieces] + [
            const((LANES, GLA_K_WIDTH)), const((1, GLA_K_WIDTH)),
            const((LANES, GLA_K_WIDTH)), const((1, GLA_K_WIDTH)),
            const((tb, tb)), const((tb, tb)), const((tb, tb)), const((tb, tb)),
        ],
        out_specs=(
            pl.BlockSpec((tb, GLA_V_WIDTH), lambda bi, i: (bi * nt + i, 0)),
            pl.BlockSpec((tb, GLA_V_WIDTH), lambda bi, i: (bi * nt + nt - 1 - i, 0)),
        ),
        scratch_shapes=[
            pltpu.VMEM((2, GLA_HEADS, GLA_DK, GLA_DV), F32),
            pltpu.VMEM((GLA_HEADS, tb, GLA_DK), F32),
        ],
        compiler_params=_params(("parallel", "arbitrary")),
    )(*([proj] * 8), w_f, b_f, w_b, b_b, tri_f.astype(BF16), tri_b.astype(BF16), tri_f, mask_b)


def _merge_kernel(att_ref, of_ref, ob_ref, g_ref, gla_ref, glb_ref, wa_ref, wg_ref, gn_ref,
                  bma_ref, bmb_ref, o_ref):
    o = of_ref[...] + ob_ref[...]
    gn = gn_ref[...]
    on = jnp.concatenate(
        [_rms(o[:, h * GLA_DV:(h + 1) * GLA_DV], gn) for h in range(GLA_HEADS)], axis=1)
    g = g_ref[...]
    gl = (on * (g * _sigmoid(g))).astype(BF16)
    bb = jnp.dot(gl, wg_ref[...], preferred_element_type=F32)
    aa = jnp.dot(att_ref[...], wa_ref[...], preferred_element_type=F32)
    ga = _sigmoid(gla_ref[...] + bma_ref[...])
    gb = _sigmoid(glb_ref[...] + bmb_ref[...])
    o_ref[...] = (ga * aa + gb * bb).astype(BF16)


def _merge(att, o_f, o_b, proj, w_attn, w_gla, gla_norm, b_merge):
    n = att.shape[0]
    tm = 256
    row = lambda w, c=0: pl.BlockSpec((tm, w), lambda i: (i, c))
    const = lambda r, w, c=0: pl.BlockSpec((r, w), lambda i: (0, c))
    return pl.pallas_call(
        _merge_kernel,
        out_shape=jax.ShapeDtypeStruct((n, D_MODEL), BF16),
        grid=(n // tm,),
        in_specs=[
            row(ATTN_WIDTH), row(GLA_V_WIDTH), row(GLA_V_WIDTH),
            row(GLA_V_WIDTH, COL_GG // GLA_V_WIDTH),
            row(D_MODEL, 0), row(D_MODEL, 1),
            const(ATTN_WIDTH, D_MODEL), const(GLA_V_WIDTH, D_MODEL), const(1, GLA_DV),
            const(1, D_MODEL, 0), const(1, D_MODEL, 1),
        ],
        out_specs=row(D_MODEL),
        compiler_params=_params(("parallel",)),
    )(att, o_f, o_b, proj, proj, proj, w_attn, w_gla, gla_norm, b_merge, b_merge)


def _outproj_kernel(x_ref, m_ref, w_ref, o_ref):
    o_ref[...] = x_ref[...] + jnp.dot(m_ref[...], w_ref[...], preferred_element_type=F32)


def _outproj(x, mixed, w_out):
    n = x.shape[0]
    tm = 512
    row = pl.BlockSpec((tm, D_MODEL), lambda i: (i, 0))
    return pl.pallas_call(
        _outproj_kernel,
        out_shape=jax.ShapeDtypeStruct((n, D_MODEL), F32),
        grid=(n // tm,),
        in_specs=[row, row, pl.BlockSpec((D_MODEL, D_MODEL), lambda i: (0, 0))],
        out_specs=row,
        compiler_params=_params(("parallel",)),
    )(x, mixed, w_out)


def _mlp_kernel(h_ref, g_ref, wu_ref, wd_ref, gf_ref, o_ref, hn_ref, *, final_norm):
    j = pl.program_id(1)

    @pl.when(j == 0)
    def _():
        h = h_ref[...]
        hn_ref[...] = _rms(h, g_ref[...]).astype(BF16)
        o_ref[...] = h

    u = jnp.maximum(jnp.dot(hn_ref[...], wu_ref[...], preferred_element_type=F32), 0.0)
    o_ref[...] += jnp.dot((u * u).astype(BF16), wd_ref[...], preferred_element_type=F32)

    if final_norm:
        @pl.when(j == pl.num_programs(1) - 1)
        def _():
            o_ref[...] = _rms(o_ref[...], gf_ref[...])


def _mlp(h, gain, w_up, w_down, gain_final, final_norm):
    n = h.shape[0]
    tm = 512
    tf = 512
    row = pl.BlockSpec((tm, D_MODEL), lambda i, j: (i, 0))
    vec = pl.BlockSpec((1, D_MODEL), lambda i, j: (0, 0))
    return pl.pallas_call(
        functools.partial(_mlp_kernel, final_norm=final_norm),
        out_shape=jax.ShapeDtypeStruct((n, D_MODEL), F32),
        grid=(n // tm, D_FF // tf),
        in_specs=[
            row, vec,
            pl.BlockSpec((D_MODEL, tf), lambda i, j: (0, j)),
            pl.BlockSpec((tf, D_MODEL), lambda i, j: (j, 0)),
            vec,
        ],
        out_specs=row,
        scratch_shapes=[pltpu.VMEM((tm, D_MODEL), BF16)],
        compiler_params=_params(("parallel", "arbitrary")),
    )(h, gain, w_up, w_down, gain_final)


def _prep_layer(w_in, w_gate_up_fwd, b_gate_fwd, w_gate_up_bwd, b_gate_bwd, w_attn_proj,
                w_gla_proj, w_out, w_up, w_down):
    offs = [0]
    for wdt in (ATTN_WIDTH, KV_WIDTH, KV_WIDTH, GLA_K_WIDTH, GLA_K_WIDTH, GLA_V_WIDTH,
                GLA_V_WIDTH, GLA_GATE_RANK, GLA_GATE_RANK, N_BRANCHES * D_MODEL):
        offs.append(offs[-1] + wdt)
    q_a, k_a, v_a, q_g, k_g, v_g, g_g, lr_f, lr_b, gate = [
        w_in[:, offs[i]:offs[i + 1]] for i in range(10)]
    pad = jnp.zeros((D_MODEL, PROJ_W - COL_LR - 2 * GLA_GATE_RANK), w_in.dtype)
    w_proj = jnp.concatenate([gate, q_a, v_g, g_g, k_a, v_a, q_g, k_g, lr_f, lr_b, pad],
                             axis=1).astype(BF16)
    zf = jnp.zeros((LANES - GLA_GATE_RANK, GLA_K_WIDTH), F32)
    zb = jnp.zeros((LANES - 2 * GLA_GATE_RANK, GLA_K_WIDTH), F32)
    w_f = jnp.concatenate([w_gate_up_fwd, zf], axis=0).astype(BF16)
    w_b = jnp.concatenate([jnp.zeros((GLA_GATE_RANK, GLA_K_WIDTH), F32), w_gate_up_bwd, zb],
                          axis=0).astype(BF16)
    return dict(
        w_proj=w_proj, w_f=w_f, w_b=w_b,
        b_f=b_gate_fwd.reshape(1, -1), b_b=b_gate_bwd.reshape(1, -1),
        w_attn=w_attn_proj.astype(BF16), w_gla=w_gla_proj.astype(BF16),
        w_out=w_out.astype(BF16), w_up=w_up.astype(BF16), w_down=w_down.astype(BF16))


def _layer(x, b, t, lw, norm_mix, q_norm, k_norm, gla_norm, b_merge, norm_mlp, norm_final,
           final_norm):
    proj = _inproj(x, norm_mix.reshape(1, -1), lw["w_proj"])
    q, kt, v = _prep(proj, _rope_tables(t), q_norm.reshape(1, -1), k_norm.reshape(1, -1), b, t)
    att = _attention(q, kt, v, b, t)
    o_f, o_b = _gla(proj, lw["w_f"], lw["b_f"], lw["w_b"], lw["b_b"], b, t)
    mixed = _merge(att, o_f, o_b, proj, lw["w_attn"], lw["w_gla"], gla_norm.reshape(1, -1),
                   b_merge.reshape(1, -1))
    h = _outproj(x, mixed, lw["w_out"])
    return _mlp(h, norm_mlp.reshape(1, -1), lw["w_up"], lw["w_down"], norm_final.reshape(1, -1),
                final_norm)


def kernel(x_prompt, x_sample, norm_mix, w_in, q_norm, k_norm, w_gate_up_fwd, b_gate_fwd,
           w_gate_up_bwd, b_gate_bwd, gla_norm, w_attn_proj, w_gla_proj, b_merge, w_out,
           norm_mlp, w_up, w_down, norm_final):
    layers = [
        _prep_layer(w_in[l], w_gate_up_fwd[l], b_gate_fwd[l], w_gate_up_bwd[l], b_gate_bwd[l],
                    w_attn_proj[l], w_gla_proj[l], w_out[l], w_up[l], w_down[l])
        for l in range(DEPTH)]

    def trunk(x):
        b, t, d = x.shape
        y = x.reshape(b * t, d)
        for l in range(DEPTH):
            y = _layer(y, b, t, layers[l], norm_mix[l], q_norm[l], k_norm[l], gla_norm[l],
                       b_merge[l], norm_mlp[l], norm_final, final_norm=(l == DEPTH - 1))
        return y.reshape(b, t, d)

    return trunk(x_prompt), trunk(x_sample)
```

```python
import functools

import jax
import jax.numpy as jnp
from jax import lax
from jax.experimental import pallas as pl
from jax.experimental.pallas import tpu as pltpu

F32 = jnp.float32
BF16 = jnp.bfloat16

D_MODEL = 2048
DEPTH = 1
GRID_W = 64
HEAD_DIM = 128
N_Q_HEADS = 8
N_KV_HEADS = 2
Q_GROUP = N_Q_HEADS // N_KV_HEADS
ATTN_WIDTH = N_Q_HEADS * HEAD_DIM
KV_WIDTH = N_KV_HEADS * HEAD_DIM
ROPE_THETA = 10000.0
GLA_HEADS = 4
GLA_DK = 128
GLA_DV = 256
GLA_K_WIDTH = GLA_HEADS * GLA_DK
GLA_V_WIDTH = GLA_HEADS * GLA_DV
GLA_GATE_RANK = 16
GLA_GATE_NORMALIZER = 16.0
GLA_CHUNK = 64
N_BRANCHES = 2
D_FF = 4 * D_MODEL
NORM_EPS = 1e-6

LANES = 128
VMEM_LIMIT = 48 * 1024 * 1024

COL_GATE = 0
COL_QA = COL_GATE + N_BRANCHES * D_MODEL
COL_VG = COL_QA + ATTN_WIDTH
COL_GG = COL_VG + GLA_V_WIDTH
COL_KVA = COL_GG + GLA_V_WIDTH
COL_QG = COL_KVA + 2 * KV_WIDTH
COL_KG = COL_QG + GLA_K_WIDTH
COL_LR = COL_KG + GLA_K_WIDTH
PROJ_TN = 1280
PROJ_W = 7 * PROJ_TN

GLA_TB = 256
ATT_TQ = 256
ATT_TK = 512
ATT_KV_UNROLL = 8
LOG2_E = 1.4426950408889634
ATT_BOUNDED_MAX_LOG2 = 60.0


def _params(sem):
    return pltpu.CompilerParams(dimension_semantics=sem, vmem_limit_bytes=VMEM_LIMIT)


def _sigmoid(x):
    return 1.0 / (1.0 + jnp.exp(-x))


def _rms(x, gain):
    ms = jnp.mean(x * x, axis=-1, keepdims=True)
    return x * lax.rsqrt(ms + NORM_EPS) * gain


def _inproj_kernel(x_ref, g_ref, w_ref, o_ref, xn_ref):
    @pl.when(pl.program_id(1) == 0)
    def _():
        xn_ref[...] = _rms(x_ref[...], g_ref[...]).astype(BF16)

    o_ref[...] = jnp.dot(xn_ref[...], w_ref[...], preferred_element_type=F32)


def _inproj(x, gain, w):
    n = x.shape[0]
    tm = 1024
    return pl.pallas_call(
        _inproj_kernel,
        out_shape=jax.ShapeDtypeStruct((n, PROJ_W), F32),
        grid=(n // tm, PROJ_W // PROJ_TN),
        in_specs=[
            pl.BlockSpec((tm, D_MODEL), lambda i, j: (i, 0)),
            pl.BlockSpec((1, D_MODEL), lambda i, j: (0, 0)),
            pl.BlockSpec((D_MODEL, PROJ_TN), lambda i, j: (0, j)),
        ],
        out_specs=pl.BlockSpec((tm, PROJ_TN), lambda i, j: (i, j)),
        scratch_shapes=[pltpu.VMEM((tm, D_MODEL), BF16)],
        compiler_params=_params(("parallel", "arbitrary")),
        name="in_proj",
    )(x, gain, w)


def _prep_kernel(q_ref, kv_ref, cos_ref, sa_ref, sb_ref, qg_ref, kg_ref, qo_ref, kt_ref, vo_ref):
    cos = cos_ref[...]
    sa = sa_ref[...]
    sb = sb_ref[...]

    def norm_rope(x, gain):
        y = _rms(x, gain)
        return y * cos + pltpu.roll(y, 96, 1) * sa + pltpu.roll(y, 32, 1) * sb

    qg = qg_ref[...]
    kg = kg_ref[...]
    for h in range(N_Q_HEADS):
        sl = slice(h * HEAD_DIM, (h + 1) * HEAD_DIM)
        qo_ref[:, sl] = (norm_rope(q_ref[:, sl], qg) * (LOG2_E * HEAD_DIM ** -0.5)).astype(BF16)
    for h in range(N_KV_HEADS):
        sl = slice(h * HEAD_DIM, (h + 1) * HEAD_DIM)
        kt_ref[0, h, 0] = norm_rope(kv_ref[:, sl], kg).T.astype(BF16)
    vo_ref[...] = kv_ref[:, KV_WIDTH:].astype(BF16)


def _prep(proj, tables, q_gain, k_gain, b, t):
    n = b * t
    tm = ATT_TK
    nt = t // tm
    cos, sa, sb = tables
    tab_spec = pl.BlockSpec((tm, HEAD_DIM), lambda i: (i % nt, 0))
    gain_spec = pl.BlockSpec((1, HEAD_DIM), lambda i: (0, 0))
    return pl.pallas_call(
        _prep_kernel,
        out_shape=(
            jax.ShapeDtypeStruct((n, ATTN_WIDTH), BF16),
            jax.ShapeDtypeStruct((b, N_KV_HEADS, nt, HEAD_DIM, tm), BF16),
            jax.ShapeDtypeStruct((n, KV_WIDTH), BF16),
        ),
        grid=(n // tm,),
        in_specs=[
            pl.BlockSpec((tm, ATTN_WIDTH), lambda i: (i, COL_QA // ATTN_WIDTH)),
            pl.BlockSpec((tm, 2 * KV_WIDTH), lambda i: (i, COL_KVA // (2 * KV_WIDTH))),
            tab_spec, tab_spec, tab_spec, gain_spec, gain_spec,
        ],
        out_specs=(
            pl.BlockSpec((tm, ATTN_WIDTH), lambda i: (i, 0)),
            pl.BlockSpec((1, N_KV_HEADS, 1, HEAD_DIM, tm), lambda i: (i // nt, 0, i % nt, 0, 0)),
            pl.BlockSpec((tm, KV_WIDTH), lambda i: (i, 0)),
        ),
        compiler_params=_params(("parallel",)),
        name="qkv_prep",
    )(proj, proj, cos, sa, sb, q_gain, k_gain)


def _rope_tables(t):
    n_rows = t // GRID_W
    sec = HEAD_DIM // 2
    inv_freq = ROPE_THETA ** (-jnp.arange(0, sec, 2, dtype=F32) / sec)
    ang_r = jnp.arange(n_rows, dtype=F32)[:, None] * inv_freq[None, :]
    ang_c = jnp.arange(GRID_W, dtype=F32)[:, None] * inv_freq[None, :]
    by_row = lambda a: jnp.repeat(a, GRID_W, axis=0)
    by_col = lambda a: jnp.tile(a, (n_rows, 1))
    cos_r, sin_r = by_row(jnp.cos(ang_r)), by_row(jnp.sin(ang_r))
    cos_c, sin_c = by_col(jnp.cos(ang_c)), by_col(jnp.sin(ang_c))
    zero = jnp.zeros_like(cos_r)
    cos = jnp.concatenate([cos_r, cos_r, cos_c, cos_c], axis=1)
    sin_a = jnp.concatenate([-sin_r, zero, -sin_c, zero], axis=1)
    sin_b = jnp.concatenate([zero, sin_r, zero, sin_c], axis=1)
    return cos, sin_a, sin_b


def _attn_kernel(bounded_ref, q_ref, kt_ref, v_ref, o_ref, acc_ref, l_ref, m_ref, *, nk):
    tq = q_ref.shape[0]
    tk = kt_ref.shape[-1]

    def q_head(g):
        return q_ref[:, g * HEAD_DIM:(g + 1) * HEAD_DIM]

    def head_rows(g):
        return slice(g * tq, (g + 1) * tq)

    def lane_partial_sum(p):
        return sum(p[:, c * LANES:(c + 1) * LANES] for c in range(tk // LANES))

    def kv_tile(j):
        return kt_ref[0, 0, j], v_ref[pl.ds(pl.multiple_of(j * tk, tk), tk), :]

    acc_ref[...] = jnp.zeros_like(acc_ref)
    l_ref[...] = jnp.zeros_like(l_ref)

    @pl.when(bounded_ref[0] != 0)
    def _():
        q = jnp.concatenate([q_head(g) for g in range(Q_GROUP)], axis=0)

        def body(j, carry):
            kt, v = kv_tile(j)
            p = jnp.exp2(jnp.dot(q, kt, preferred_element_type=F32))
            l_ref[...] += lane_partial_sum(p)
            acc_ref[...] += jnp.dot(p.astype(BF16), v, preferred_element_type=F32)
            return carry

        lax.fori_loop(0, nk, body, 0, unroll=ATT_KV_UNROLL)

    @pl.when(bounded_ref[0] == 0)
    def _():
        m_ref[...] = jnp.full_like(m_ref, -jnp.inf)

        def body(j, carry):
            kt, v = kv_tile(j)
            for g in range(Q_GROUP):
                rows = head_rows(g)
                s = jnp.dot(q_head(g), kt, preferred_element_type=F32)
                m_old = m_ref[rows]
                m_new = jnp.maximum(m_old, jnp.max(s, axis=-1, keepdims=True))
                alpha = jnp.exp2(m_old - m_new)
                p = jnp.exp2(s - m_new)
                m_ref[rows] = m_new
                l_ref[rows] = alpha * l_ref[rows] + lane_partial_sum(p)
                acc_ref[rows] = alpha * acc_ref[rows] + jnp.dot(p.astype(BF16), v,
                                                                preferred_element_type=F32)
            return carry

        lax.fori_loop(0, nk, body, 0)

    for g in range(Q_GROUP):
        rows = head_rows(g)
        l = jnp.sum(l_ref[rows], axis=-1, keepdims=True)
        o_ref[:, g * HEAD_DIM:(g + 1) * HEAD_DIM] = (acc_ref[rows] * (1.0 / l)).astype(BF16)


def _scores_bounded(q_gain, k_gain):
    bound = (1.02 * LOG2_E * HEAD_DIM ** 0.5) * jnp.max(jnp.abs(q_gain)) * jnp.max(jnp.abs(k_gain))
    return (bound <= ATT_BOUNDED_MAX_LOG2).astype(jnp.int32).reshape(1)


def _attention(q, kt, v, bounded, b, t):
    n = b * t
    tq = ATT_TQ
    nq = t // tq
    nk = t // ATT_TK
    gw = Q_GROUP * HEAD_DIM
    return pl.pallas_call(
        functools.partial(_attn_kernel, nk=nk),
        out_shape=jax.ShapeDtypeStruct((n, ATTN_WIDTH), BF16),
        grid=(b, N_KV_HEADS, nq),
        in_specs=[
            pl.BlockSpec(memory_space=pltpu.SMEM),
            pl.BlockSpec((tq, gw), lambda bi, h, i: (bi * nq + i, h)),
            pl.BlockSpec((1, 1, nk, HEAD_DIM, ATT_TK), lambda bi, h, i: (bi, h, 0, 0, 0)),
            pl.BlockSpec((t, HEAD_DIM), lambda bi, h, i: (bi, h)),
        ],
        out_specs=pl.BlockSpec((tq, gw), lambda bi, h, i: (bi * nq + i, h)),
        scratch_shapes=[
            pltpu.VMEM((Q_GROUP * tq, HEAD_DIM), F32),
            pltpu.VMEM((Q_GROUP * tq, LANES), F32),
            pltpu.VMEM((Q_GROUP * tq, 1), F32),
        ],
        compiler_params=_params(("parallel", "parallel", "arbitrary")),
        name="attention",
    )(bounded, q, kt, v)


def _gla_dir(q_ref, k_ref, v_ref, lr_ref, w_ref, b_ref, tri_ref, mask_ref, o_ref, s_ref, cs_ref,
             reverse):
    c_sz = GLA_CHUNK
    n_chunks = GLA_TB // c_sz
    z = jnp.dot(lr_ref[...].astype(BF16), w_ref[...], preferred_element_type=F32) + b_ref[...]
    la = (jnp.minimum(z, 0.0) - jnp.log(1.0 + jnp.exp(-jnp.abs(z)))) * (1.0 / GLA_GATE_NORMALIZER)
    hi = la.astype(BF16)
    lo = (la - hi.astype(F32)).astype(BF16)
    tri = tri_ref[...]
    cs = (jnp.dot(tri, hi, preferred_element_type=F32)
          + jnp.dot(tri, lo, preferred_element_type=F32))
    for h in range(GLA_HEADS):
        cs_ref[h] = cs[:, h * GLA_DK:(h + 1) * GLA_DK]
    half = c_sz // 2
    edge_start = 0 if reverse else half - 1
    edge_row = [2 * c if reverse else 2 * c + 1 for c in range(n_chunks)]

    qe = (q_ref[...] * (GLA_DK ** -0.5)) * jnp.exp(cs)
    ke = k_ref[...] * jnp.exp(-cs)
    qe_b = qe.astype(BF16)
    ke_b = ke.astype(BF16)
    v_b = v_ref[...].astype(BF16)
    mask = mask_ref[...] > 0.5
    chunk_of_lane = lax.broadcasted_iota(jnp.int32, (GLA_DK, GLA_TB), 1) // c_sz
    order = range(n_chunks - 1, -1, -1) if reverse else range(n_chunks)

    for h in range(GLA_HEADS):
        ks = slice(h * GLA_DK, (h + 1) * GLA_DK)
        vs = slice(h * GLA_DV, (h + 1) * GLA_DV)
        a = lax.dot_general(qe_b[:, ks], ke_b[:, ks], (((1,), (1,)), ((), ())),
                            preferred_element_type=F32)
        a = jnp.where(mask, a, 0.0).astype(BF16)
        o_intra = jnp.dot(a, v_b[:, vs], preferred_element_type=F32)
        edge = jnp.exp(cs_ref[h, pl.ds(edge_start, 2 * n_chunks, stride=half), :])
        decay_t = jnp.tile(edge, (GLA_DK // (2 * n_chunks), 1)).T
        kd = ke[:, ks] * jnp.concatenate(
            [jnp.broadcast_to(edge[r:r + 1], (c_sz, GLA_DK)) for r in edge_row], axis=0)
        kd_t = kd.T
        s = s_ref[h]
        for c in order:
            rs = slice(c * c_sz, (c + 1) * c_sz)
            r = edge_row[c]
            o_inter = jnp.dot(qe_b[rs, ks], s.astype(BF16), preferred_element_type=F32)
            o_ref[rs, vs] = o_intra[rs] + o_inter
            kd_c = jnp.where(chunk_of_lane == c, kd_t, 0.0).astype(BF16)
            s = decay_t[:, r:r + 1] * s + jnp.dot(kd_c, v_b[:, vs], preferred_element_type=F32)
        s_ref[h] = s


def _gla_kernel(qf_ref, kf_ref, vf_ref, lrf_ref, qb_ref, kb_ref, vb_ref, lrb_ref,
                wf_ref, bf_ref, wb_ref, bb_ref, trif_ref, trib_ref, maskf_ref, maskb_ref,
                of_ref, ob_ref, s_ref, cs_ref):
    @pl.when(pl.program_id(1) == 0)
    def _():
        s_ref[...] = jnp.zeros_like(s_ref)

    _gla_dir(qf_ref, kf_ref, vf_ref, lrf_ref, wf_ref, bf_ref, trif_ref, maskf_ref, of_ref,
             s_ref.at[0], cs_ref, reverse=False)
    _gla_dir(qb_ref, kb_ref, vb_ref, lrb_ref, wb_ref, bb_ref, trib_ref, maskb_ref, ob_ref,
             s_ref.at[1], cs_ref, reverse=True)


def _gla_consts():
    i = jnp.arange(GLA_TB)[:, None]
    j = jnp.arange(GLA_TB)[None, :]
    same = (i // GLA_CHUNK) == (j // GLA_CHUNK)
    tri_f = (same & (j <= i)).astype(F32)
    tri_b = (same & (j >= i)).astype(F32)
    mask_b = (same & (j > i)).astype(F32)
    return tri_f, tri_b, mask_b


def _gla(proj, w_f, b_f, w_b, b_b, b, t):
    n = b * t
    tb = GLA_TB
    nt = t // tb
    tri_f, tri_b, mask_b = _gla_consts()

    def fwd(width, col):
        return pl.BlockSpec((tb, width), lambda bi, i: (bi * nt + i, col // width))

    def bwd(width, col):
        return pl.BlockSpec((tb, width), lambda bi, i: (bi * nt + nt - 1 - i, col // width))

    def const(shape):
        return pl.BlockSpec(shape, lambda bi, i: (0, 0))

    pieces = ((GLA_K_WIDTH, COL_QG), (GLA_K_WIDTH, COL_KG), (GLA_V_WIDTH, COL_VG), (LANES, COL_LR))
    out_sd = jax.ShapeDtypeStruct((n, GLA_V_WIDTH), F32)
    return pl.pallas_call(
        _gla_kernel,
        out_shape=(out_sd, out_sd),
        grid=(b, nt),
        in_specs=[fwd(*p) for p in pieces] + [bwd(*p) for p in pieces] + [
            const((LANES, GLA_K_WIDTH)), const((1, GLA_K_WIDTH)),
            const((LANES, GLA_K_WIDTH)), const((1, GLA_K_WIDTH)),
            const((tb, tb)), const((tb, tb)), const((tb, tb)), const((tb, tb)),
        ],
        out_specs=(
            pl.BlockSpec((tb, GLA_V_WIDTH), lambda bi, i: (bi * nt + i, 0)),
            pl.BlockSpec((tb, GLA_V_WIDTH), lambda bi, i: (bi * nt + nt - 1 - i, 0)),
        ),
        scratch_shapes=[
            pltpu.VMEM((2, GLA_HEADS, GLA_DK, GLA_DV), F32),
            pltpu.VMEM((GLA_HEADS, tb, GLA_DK), F32),
        ],
        compiler_params=_params(("parallel", "arbitrary")),
        name="gla",
    )(*([proj] * 8), w_f, b_f, w_b, b_b, tri_f.astype(BF16), tri_b.astype(BF16), tri_f, mask_b)


def _merge_kernel(att_ref, of_ref, ob_ref, g_ref, gla_ref, glb_ref, wa_ref, wg_ref, gn_ref,
                  bma_ref, bmb_ref, o_ref):
    o = of_ref[...] + ob_ref[...]
    gn = gn_ref[...]
    on = jnp.concatenate(
        [_rms(o[:, h * GLA_DV:(h + 1) * GLA_DV], gn) for h in range(GLA_HEADS)], axis=1)
    g = g_ref[...]
    gl = (on * (g * _sigmoid(g))).astype(BF16)
    bb = jnp.dot(gl, wg_ref[...], preferred_element_type=F32)
    aa = jnp.dot(att_ref[...], wa_ref[...], preferred_element_type=F32)
    ga = _sigmoid(gla_ref[...] + bma_ref[...])
    gb = _sigmoid(glb_ref[...] + bmb_ref[...])
    o_ref[...] = (ga * aa + gb * bb).astype(BF16)


def _merge(att, o_f, o_b, proj, w_attn, w_gla, gla_norm, b_merge):
    n = att.shape[0]
    tm = 256
    row = lambda w, c=0: pl.BlockSpec((tm, w), lambda i: (i, c))
    const = lambda r, w, c=0: pl.BlockSpec((r, w), lambda i: (0, c))
    return pl.pallas_call(
        _merge_kernel,
        out_shape=jax.ShapeDtypeStruct((n, D_MODEL), BF16),
        grid=(n // tm,),
        in_specs=[
            row(ATTN_WIDTH), row(GLA_V_WIDTH), row(GLA_V_WIDTH),
            row(GLA_V_WIDTH, COL_GG // GLA_V_WIDTH),
            row(D_MODEL, 0), row(D_MODEL, 1),
            const(ATTN_WIDTH, D_MODEL), const(GLA_V_WIDTH, D_MODEL), const(1, GLA_DV),
            const(1, D_MODEL, 0), const(1, D_MODEL, 1),
        ],
        out_specs=row(D_MODEL),
        compiler_params=_params(("parallel",)),
        name="merge",
    )(att, o_f, o_b, proj, proj, proj, w_attn, w_gla, gla_norm, b_merge, b_merge)


def _outproj_kernel(x_ref, m_ref, w_ref, o_ref):
    o_ref[...] = x_ref[...] + jnp.dot(m_ref[...], w_ref[...], preferred_element_type=F32)


def _outproj(x, mixed, w_out):
    n = x.shape[0]
    tm = 512
    row = pl.BlockSpec((tm, D_MODEL), lambda i: (i, 0))
    return pl.pallas_call(
        _outproj_kernel,
        out_shape=jax.ShapeDtypeStruct((n, D_MODEL), F32),
        grid=(n // tm,),
        in_specs=[row, row, pl.BlockSpec((D_MODEL, D_MODEL), lambda i: (0, 0))],
        out_specs=row,
        compiler_params=_params(("parallel",)),
        name="out_proj",
    )(x, mixed, w_out)


def _mlp_kernel(h_ref, g_ref, wu_ref, wd_ref, gf_ref, o_ref, hn_ref, *, final_norm):
    j = pl.program_id(1)

    @pl.when(j == 0)
    def _():
        h = h_ref[...]
        hn_ref[...] = _rms(h, g_ref[...]).astype(BF16)
        o_ref[...] = h

    u = jnp.maximum(jnp.dot(hn_ref[...], wu_ref[...], preferred_element_type=F32), 0.0)
    o_ref[...] += jnp.dot((u * u).astype(BF16), wd_ref[...], preferred_element_type=F32)

    if final_norm:
        @pl.when(j == pl.num_programs(1) - 1)
        def _():
            o_ref[...] = _rms(o_ref[...], gf_ref[...])


def _mlp(h, gain, w_up, w_down, gain_final, final_norm):
    n = h.shape[0]
    tm = 512
    tf = 512
    row = pl.BlockSpec((tm, D_MODEL), lambda i, j: (i, 0))
    vec = pl.BlockSpec((1, D_MODEL), lambda i, j: (0, 0))
    return pl.pallas_call(
        functools.partial(_mlp_kernel, final_norm=final_norm),
        out_shape=jax.ShapeDtypeStruct((n, D_MODEL), F32),
        grid=(n // tm, D_FF // tf),
        in_specs=[
            row, vec,
            pl.BlockSpec((D_MODEL, tf), lambda i, j: (0, j)),
            pl.BlockSpec((tf, D_MODEL), lambda i, j: (j, 0)),
            vec,
        ],
        out_specs=row,
        scratch_shapes=[pltpu.VMEM((tm, D_MODEL), BF16)],
        compiler_params=_params(("parallel", "arbitrary")),
        name="mlp",
    )(h, gain, w_up, w_down, gain_final)


def _prep_layer(w_in, w_gate_up_fwd, b_gate_fwd, w_gate_up_bwd, b_gate_bwd, w_attn_proj,
                w_gla_proj, w_out, w_up, w_down):
    offs = [0]
    for wdt in (ATTN_WIDTH, KV_WIDTH, KV_WIDTH, GLA_K_WIDTH, GLA_K_WIDTH, GLA_V_WIDTH,
                GLA_V_WIDTH, GLA_GATE_RANK, GLA_GATE_RANK, N_BRANCHES * D_MODEL):
        offs.append(offs[-1] + wdt)
    q_a, k_a, v_a, q_g, k_g, v_g, g_g, lr_f, lr_b, gate = [
        w_in[:, offs[i]:offs[i + 1]] for i in range(10)]
    pad = jnp.zeros((D_MODEL, PROJ_W - COL_LR - 2 * GLA_GATE_RANK), w_in.dtype)
    w_proj = jnp.concatenate([gate, q_a, v_g, g_g, k_a, v_a, q_g, k_g, lr_f, lr_b, pad],
                             axis=1).astype(BF16)
    zf = jnp.zeros((LANES - GLA_GATE_RANK, GLA_K_WIDTH), F32)
    zb = jnp.zeros((LANES - 2 * GLA_GATE_RANK, GLA_K_WIDTH), F32)
    w_f = jnp.concatenate([w_gate_up_fwd, zf], axis=0).astype(BF16)
    w_b = jnp.concatenate([jnp.zeros((GLA_GATE_RANK, GLA_K_WIDTH), F32), w_gate_up_bwd, zb],
                          axis=0).astype(BF16)
    return dict(
        w_proj=w_proj, w_f=w_f, w_b=w_b,
        b_f=b_gate_fwd.reshape(1, -1), b_b=b_gate_bwd.reshape(1, -1),
        w_attn=w_attn_proj.astype(BF16), w_gla=w_gla_proj.astype(BF16),
        w_out=w_out.astype(BF16), w_up=w_up.astype(BF16), w_down=w_down.astype(BF16))


def _layer(x, b, t, lw, norm_mix, q_norm, k_norm, gla_norm, b_merge, norm_mlp, norm_final,
           final_norm):
    proj = _inproj(x, norm_mix.reshape(1, -1), lw["w_proj"])
    q, kt, v = _prep(proj, _rope_tables(t), q_norm.reshape(1, -1), k_norm.reshape(1, -1), b, t)
    att = _attention(q, kt, v, _scores_bounded(q_norm, k_norm), b, t)
    o_f, o_b = _gla(proj, lw["w_f"], lw["b_f"], lw["w_b"], lw["b_b"], b, t)
    mixed = _merge(att, o_f, o_b, proj, lw["w_attn"], lw["w_gla"], gla_norm.reshape(1, -1),
                   b_merge.reshape(1, -1))
    h = _outproj(x, mixed, lw["w_out"])
    return _mlp(h, norm_mlp.reshape(1, -1), lw["w_up"], lw["w_down"], norm_final.reshape(1, -1),
                final_norm)


def kernel(x_prompt, x_sample, norm_mix, w_in, q_norm, k_norm, w_gate_up_fwd, b_gate_fwd,
           w_gate_up_bwd, b_gate_bwd, gla_norm, w_attn_proj, w_gla_proj, b_merge, w_out,
           norm_mlp, w_up, w_down, norm_final):
    layers = [
        _prep_layer(w_in[l], w_gate_up_fwd[l], b_gate_fwd[l], w_gate_up_bwd[l], b_gate_bwd[l],
                    w_attn_proj[l], w_gla_proj[l], w_out[l], w_up[l], w_down[l])
        for l in range(DEPTH)]

    def trunk(x):
        b, t, d = x.shape
        y = x.reshape(b * t, d)
        for l in range(DEPTH):
            y = _layer(y, b, t, layers[l], norm_mix[l], q_norm[l], k_norm[l], gla_norm[l],
                       b_merge[l], norm_mlp[l], norm_final, final_norm=(l == DEPTH - 1))
        return y.reshape(b, t, d)

    return trunk(x_prompt), trunk(x_sample)
```

```python
import functools

import jax
import jax.numpy as jnp
from jax import lax
from jax.experimental import pallas as pl
from jax.experimental.pallas import tpu as pltpu

F32 = jnp.float32
BF16 = jnp.bfloat16

D_MODEL = 2048
DEPTH = 1
GRID_W = 64
HEAD_DIM = 128
N_Q_HEADS = 8
N_KV_HEADS = 2
Q_GROUP = N_Q_HEADS // N_KV_HEADS
ATTN_WIDTH = N_Q_HEADS * HEAD_DIM
KV_WIDTH = N_KV_HEADS * HEAD_DIM
ROPE_THETA = 10000.0
GLA_HEADS = 4
GLA_DK = 128
GLA_DV = 256
GLA_K_WIDTH = GLA_HEADS * GLA_DK
GLA_V_WIDTH = GLA_HEADS * GLA_DV
GLA_GATE_RANK = 16
GLA_GATE_NORMALIZER = 16.0
GLA_CHUNK = 64
N_BRANCHES = 2
D_FF = 4 * D_MODEL
NORM_EPS = 1e-6

LANES = 128
SUBLANES = 8
VMEM_LIMIT = 48 * 1024 * 1024

COL_GATE = 0
COL_QA = COL_GATE + N_BRANCHES * D_MODEL
COL_VG = COL_QA + ATTN_WIDTH
COL_GG = COL_VG + GLA_V_WIDTH
COL_KVA = COL_GG + GLA_V_WIDTH
COL_QG = COL_KVA + 2 * KV_WIDTH
COL_KG = COL_QG + GLA_K_WIDTH
COL_LR = COL_KG + GLA_K_WIDTH
PROJ_TN = 1280
PROJ_W = 7 * PROJ_TN

GLA_TB = 256
MERGE_TN = D_MODEL // 2
ATT_TQ = 256
ATT_TK = 512
ATT_KV_UNROLL = 8
LOG2_E = 1.4426950408889634
ATT_BOUNDED_MAX_LOG2 = 60.0


def _params(sem):
    return pltpu.CompilerParams(dimension_semantics=sem, vmem_limit_bytes=VMEM_LIMIT)


def _sigmoid(x):
    return 1.0 / (1.0 + jnp.exp(-x))


def _rms(x, gain):
    ms = jnp.mean(x * x, axis=-1, keepdims=True)
    return x * lax.rsqrt(ms + NORM_EPS) * gain


def _inproj_kernel(x_ref, g_ref, w_ref, o_ref, xn_ref):
    @pl.when(pl.program_id(1) == 0)
    def _():
        xn_ref[...] = _rms(x_ref[...], g_ref[...]).astype(BF16)

    o_ref[...] = jnp.dot(xn_ref[...], w_ref[...], preferred_element_type=F32).astype(o_ref.dtype)


def _inproj(x, gain, w):
    n = x.shape[0]
    tm = 1024
    return pl.pallas_call(
        _inproj_kernel,
        out_shape=jax.ShapeDtypeStruct((n, PROJ_W), BF16),
        grid=(n // tm, PROJ_W // PROJ_TN),
        in_specs=[
            pl.BlockSpec((tm, D_MODEL), lambda i, j: (i, 0)),
            pl.BlockSpec((1, D_MODEL), lambda i, j: (0, 0)),
            pl.BlockSpec((D_MODEL, PROJ_TN), lambda i, j: (0, j)),
        ],
        out_specs=pl.BlockSpec((tm, PROJ_TN), lambda i, j: (i, j)),
        scratch_shapes=[pltpu.VMEM((tm, D_MODEL), BF16)],
        compiler_params=_params(("parallel", "arbitrary")),
        name="in_proj",
    )(x, gain, w)


def _prep_kernel(q_ref, kv_ref, cos_ref, sa_ref, sb_ref, qg_ref, kg_ref, qo_ref, ko_ref, vt_ref):
    cos = cos_ref[...]
    sa = sa_ref[...]
    sb = sb_ref[...]

    def norm_rope(x, gain):
        y = _rms(x.astype(F32), gain)
        return y * cos + pltpu.roll(y, 96, 1) * sa + pltpu.roll(y, 32, 1) * sb

    qg = qg_ref[...]
    kg = kg_ref[...]
    for h in range(N_Q_HEADS):
        sl = slice(h * HEAD_DIM, (h + 1) * HEAD_DIM)
        qo_ref[:, sl] = (norm_rope(q_ref[:, sl], qg) * (LOG2_E * HEAD_DIM ** -0.5)).astype(BF16)
    for h in range(N_KV_HEADS):
        sl = slice(h * HEAD_DIM, (h + 1) * HEAD_DIM)
        ko_ref[:, sl] = norm_rope(kv_ref[:, sl], kg).astype(BF16)
        v = kv_ref[:, KV_WIDTH + h * HEAD_DIM:KV_WIDTH + (h + 1) * HEAD_DIM]
        vt_ref[0, h, 0] = v.astype(F32).T.astype(BF16)


def _prep(proj, tables, q_gain, k_gain, b, t):
    n = b * t
    tm = ATT_TK
    nt = t // tm
    cos, sa, sb = tables
    tab_spec = pl.BlockSpec((tm, HEAD_DIM), lambda i: (i % nt, 0))
    gain_spec = pl.BlockSpec((1, HEAD_DIM), lambda i: (0, 0))
    return pl.pallas_call(
        _prep_kernel,
        out_shape=(
            jax.ShapeDtypeStruct((n, ATTN_WIDTH), BF16),
            jax.ShapeDtypeStruct((n, KV_WIDTH), BF16),
            jax.ShapeDtypeStruct((b, N_KV_HEADS, nt, HEAD_DIM, tm), BF16),
        ),
        grid=(n // tm,),
        in_specs=[
            pl.BlockSpec((tm, ATTN_WIDTH), lambda i: (i, COL_QA // ATTN_WIDTH)),
            pl.BlockSpec((tm, 2 * KV_WIDTH), lambda i: (i, COL_KVA // (2 * KV_WIDTH))),
            tab_spec, tab_spec, tab_spec, gain_spec, gain_spec,
        ],
        out_specs=(
            pl.BlockSpec((tm, ATTN_WIDTH), lambda i: (i, 0)),
            pl.BlockSpec((tm, KV_WIDTH), lambda i: (i, 0)),
            pl.BlockSpec((1, N_KV_HEADS, 1, HEAD_DIM, tm), lambda i: (i // nt, 0, i % nt, 0, 0)),
        ),
        compiler_params=_params(("parallel",)),
        name="qkv_prep",
    )(proj, proj, cos, sa, sb, q_gain, k_gain)


def _rope_tables(t):
    n_rows = t // GRID_W
    sec = HEAD_DIM // 2
    inv_freq = ROPE_THETA ** (-jnp.arange(0, sec, 2, dtype=F32) / sec)
    ang_r = jnp.arange(n_rows, dtype=F32)[:, None] * inv_freq[None, :]
    ang_c = jnp.arange(GRID_W, dtype=F32)[:, None] * inv_freq[None, :]
    by_row = lambda a: jnp.repeat(a, GRID_W, axis=0)
    by_col = lambda a: jnp.tile(a, (n_rows, 1))
    cos_r, sin_r = by_row(jnp.cos(ang_r)), by_row(jnp.sin(ang_r))
    cos_c, sin_c = by_col(jnp.cos(ang_c)), by_col(jnp.sin(ang_c))
    zero = jnp.zeros_like(cos_r)
    cos = jnp.concatenate([cos_r, cos_r, cos_c, cos_c], axis=1)
    sin_a = jnp.concatenate([-sin_r, zero, -sin_c, zero], axis=1)
    sin_b = jnp.concatenate([zero, sin_r, zero, sin_c], axis=1)
    return cos, sin_a, sin_b


def _attn_kernel(bounded_ref, q_ref, k_ref, vt_ref, o_ref, acc_ref, l_ref, m_ref, *, nk):
    tq = q_ref.shape[0]
    tk = vt_ref.shape[-1]
    cols = Q_GROUP * tq
    sub = l_ref.shape[0]

    qt = jnp.concatenate(
        [q_ref[:, g * HEAD_DIM:(g + 1) * HEAD_DIM].astype(F32).T.astype(BF16)
         for g in range(Q_GROUP)], axis=1)

    def kv_tile(j):
        return k_ref[pl.ds(pl.multiple_of(j * tk, tk), tk), :], vt_ref[0, 0, j]

    def sublane_partial_sum(p):
        return jnp.sum(p.reshape(tk // sub, sub, cols), axis=0)

    acc_ref[...] = jnp.zeros_like(acc_ref)
    l_ref[...] = jnp.zeros_like(l_ref)

    @pl.when(bounded_ref[0] != 0)
    def _():
        def body(j, carry):
            k, vt = kv_tile(j)
            p = jnp.exp2(jnp.dot(k, qt, preferred_element_type=F32))
            l_ref[...] += sublane_partial_sum(p)
            acc_ref[...] += jnp.dot(vt, p.astype(BF16), preferred_element_type=F32)
            return carry

        lax.fori_loop(0, nk, body, 0, unroll=ATT_KV_UNROLL)

    @pl.when(bounded_ref[0] == 0)
    def _():
        m_ref[...] = jnp.full_like(m_ref, -jnp.inf)

        def body(j, carry):
            k, vt = kv_tile(j)
            s = jnp.dot(k, qt, preferred_element_type=F32)
            m_old = m_ref[...]
            m_new = jnp.maximum(m_old, jnp.max(s, axis=0, keepdims=True))
            alpha = jnp.exp2(m_old - m_new)
            p = jnp.exp2(s - m_new)
            m_ref[...] = m_new
            l_ref[...] = alpha * l_ref[...] + sublane_partial_sum(p)
            acc_ref[...] = alpha * acc_ref[...] + jnp.dot(vt, p.astype(BF16),
                                                          preferred_element_type=F32)
            return carry

        lax.fori_loop(0, nk, body, 0)

    out_t = acc_ref[...] * (1.0 / jnp.sum(l_ref[...], axis=0, keepdims=True))
    for g in range(Q_GROUP):
        o_ref[:, g * HEAD_DIM:(g + 1) * HEAD_DIM] = out_t[:, g * tq:(g + 1) * tq].T.astype(BF16)


def _scores_bounded(q_gain, k_gain):
    bound = (1.02 * LOG2_E * HEAD_DIM ** 0.5) * jnp.max(jnp.abs(q_gain)) * jnp.max(jnp.abs(k_gain))
    return (bound <= ATT_BOUNDED_MAX_LOG2).astype(jnp.int32).reshape(1)


def _attention(q, k, vt, bounded, b, t):
    n = b * t
    tq = ATT_TQ
    nq = t // tq
    nk = t // ATT_TK
    gw = Q_GROUP * HEAD_DIM
    return pl.pallas_call(
        functools.partial(_attn_kernel, nk=nk),
        out_shape=jax.ShapeDtypeStruct((n, ATTN_WIDTH), BF16),
        grid=(b, N_KV_HEADS, nq),
        in_specs=[
            pl.BlockSpec(memory_space=pltpu.SMEM),
            pl.BlockSpec((tq, gw), lambda bi, h, i: (bi * nq + i, h)),
            pl.BlockSpec((t, HEAD_DIM), lambda bi, h, i: (bi, h)),
            pl.BlockSpec((1, 1, nk, HEAD_DIM, ATT_TK), lambda bi, h, i: (bi, h, 0, 0, 0)),
        ],
        out_specs=pl.BlockSpec((tq, gw), lambda bi, h, i: (bi * nq + i, h)),
        scratch_shapes=[
            pltpu.VMEM((HEAD_DIM, Q_GROUP * tq), F32),
            pltpu.VMEM((SUBLANES, Q_GROUP * tq), F32),
            pltpu.VMEM((1, Q_GROUP * tq), F32),
        ],
        compiler_params=_params(("parallel", "parallel", "arbitrary")),
        name="attention",
    )(bounded, q, k, vt)


def _gla_dir(q_ref, k_ref, v_ref, lr_ref, w_ref, b_ref, tri_ref, mask_ref, o_ref, s_ref, cs_ref,
             reverse):
    c_sz = GLA_CHUNK
    n_chunks = GLA_TB // c_sz
    z = jnp.dot(lr_ref[...], w_ref[...], preferred_element_type=F32) + b_ref[...]
    softplus2 = jnp.log2(1.0 + jnp.exp2(jnp.abs(z) * (-LOG2_E)))
    la = (jnp.minimum(z, 0.0) * (LOG2_E / GLA_GATE_NORMALIZER)
          - softplus2 * (1.0 / GLA_GATE_NORMALIZER))
    hi = la.astype(BF16)
    lo = (la - hi.astype(F32)).astype(BF16)
    tri = tri_ref[...]
    cs = (jnp.dot(tri, hi, preferred_element_type=F32)
          + jnp.dot(tri, lo, preferred_element_type=F32))
    for h in range(GLA_HEADS):
        cs_ref[h] = cs[:, h * GLA_DK:(h + 1) * GLA_DK]
    half = c_sz // 2
    edge_start = 0 if reverse else half - 1
    edge_row = [2 * c if reverse else 2 * c + 1 for c in range(n_chunks)]

    qe = (q_ref[...].astype(F32) * (GLA_DK ** -0.5)) * jnp.exp2(cs)
    ke = k_ref[...].astype(F32) * jnp.exp2(-cs)
    qe_b = qe.astype(BF16)
    ke_b = ke.astype(BF16)
    v_b = v_ref[...]
    mask = mask_ref[...] > 0.5
    chunk_of_lane = lax.broadcasted_iota(jnp.int32, (GLA_DK, GLA_TB), 1) // c_sz
    order = range(n_chunks - 1, -1, -1) if reverse else range(n_chunks)

    def head(h):
        ks = slice(h * GLA_DK, (h + 1) * GLA_DK)
        vs = slice(h * GLA_DV, (h + 1) * GLA_DV)
        a = lax.dot_general(qe_b[:, ks], ke_b[:, ks], (((1,), (1,)), ((), ())),
                            preferred_element_type=F32)
        a = jnp.where(mask, a, 0.0).astype(BF16)
        edge = jnp.exp2(cs_ref[h, pl.ds(edge_start, 2 * n_chunks, stride=half), :])
        decay_t = jnp.tile(edge, (GLA_DK // (2 * n_chunks), 1)).T
        kd = ke[:, ks] * jnp.concatenate(
            [jnp.broadcast_to(edge[r:r + 1], (c_sz, GLA_DK)) for r in edge_row], axis=0)
        kd_t = kd.T
        lhs = jnp.concatenate(
            [a] + [jnp.where(chunk_of_lane == c, kd_t, 0.0).astype(BF16) for c in range(n_chunks)],
            axis=0)
        prod = jnp.dot(lhs, v_b[:, vs], preferred_element_type=F32)
        s = s_ref[h]
        for c in order:
            rs = slice(c * c_sz, (c + 1) * c_sz)
            r = edge_row[c]
            o_inter = jnp.dot(qe_b[rs, ks], s.astype(BF16), preferred_element_type=F32)
            o_ref[rs, vs] = prod[rs] + o_inter
            inc = prod[GLA_TB + c * GLA_DK:GLA_TB + (c + 1) * GLA_DK]
            s = decay_t[:, r:r + 1] * s + inc
        s_ref[h] = s

    for h in range(GLA_HEADS):
        head(h)


def _gla_kernel(qf_ref, kf_ref, vf_ref, lrf_ref, qb_ref, kb_ref, vb_ref, lrb_ref,
                wf_ref, bf_ref, wb_ref, bb_ref, trif_ref, trib_ref, maskf_ref, maskb_ref,
                of_ref, ob_ref, sf_ref, sb_ref, csf_ref, csb_ref):
    @pl.when(pl.program_id(1) == 0)
    def _():
        sf_ref[...] = jnp.zeros_like(sf_ref)
        sb_ref[...] = jnp.zeros_like(sb_ref)

    _gla_dir(qf_ref, kf_ref, vf_ref, lrf_ref, wf_ref, bf_ref, trif_ref, maskf_ref, of_ref,
             sf_ref, csf_ref, reverse=False)
    _gla_dir(qb_ref, kb_ref, vb_ref, lrb_ref, wb_ref, bb_ref, trib_ref, maskb_ref, ob_ref,
             sb_ref, csb_ref, reverse=True)


def _gla_consts():
    i = jnp.arange(GLA_TB)[:, None]
    j = jnp.arange(GLA_TB)[None, :]
    same = (i // GLA_CHUNK) == (j // GLA_CHUNK)
    tri_f = (same & (j <= i)).astype(F32)
    tri_b = (same & (j >= i)).astype(F32)
    mask_b = (same & (j > i)).astype(F32)
    return tri_f, tri_b, mask_b


def _gla(proj, w_f, b_f, w_b, b_b, b, t):
    n = b * t
    tb = GLA_TB
    nt = t // tb
    tri_f, tri_b, mask_b = _gla_consts()

    def fwd(width, col):
        return pl.BlockSpec((tb, width), lambda bi, i: (bi * nt + i, col // width))

    def bwd(width, col):
        return pl.BlockSpec((tb, width), lambda bi, i: (bi * nt + nt - 1 - i, col // width))

    def const(shape):
        return pl.BlockSpec(shape, lambda bi, i: (0, 0))

    pieces = ((GLA_K_WIDTH, COL_QG), (GLA_K_WIDTH, COL_KG), (GLA_V_WIDTH, COL_VG), (LANES, COL_LR))
    out_sd = jax.ShapeDtypeStruct((n, GLA_V_WIDTH), F32)
    return pl.pallas_call(
        _gla_kernel,
        out_shape=(out_sd, out_sd),
        grid=(b, nt),
        in_specs=[fwd(*p) for p in pieces] + [bwd(*p) for p in pieces] + [
            const((LANES, GLA_K_WIDTH)), const((1, GLA_K_WIDTH)),
            const((LANES, GLA_K_WIDTH)), const((1, GLA_K_WIDTH)),
            const((tb, tb)), const((tb, tb)), const((tb, tb)), const((tb, tb)),
        ],
        out_specs=(
            pl.BlockSpec((tb, GLA_V_WIDTH), lambda bi, i: (bi * nt + i, 0)),
            pl.BlockSpec((tb, GLA_V_WIDTH), lambda bi, i: (bi * nt + nt - 1 - i, 0)),
        ),
        scratch_shapes=[
            pltpu.VMEM((GLA_HEADS, GLA_DK, GLA_DV), F32),
            pltpu.VMEM((GLA_HEADS, GLA_DK, GLA_DV), F32),
            pltpu.VMEM((GLA_HEADS, tb, GLA_DK), F32),
            pltpu.VMEM((GLA_HEADS, tb, GLA_DK), F32),
        ],
        compiler_params=_params(("parallel", "arbitrary")),
        name="gla",
    )(*([proj] * 8), w_f, b_f, w_b, b_b, tri_f.astype(BF16), tri_b.astype(BF16), tri_f, mask_b)


def _merge_kernel(att_ref, of_ref, ob_ref, g_ref, gla_ref, glb_ref, wa_ref, wg_ref, gn_ref,
                  bma_ref, bmb_ref, o_ref, gl_ref):
    j = pl.program_id(1)

    @pl.when(j == 0)
    def _():
        o = of_ref[...] + ob_ref[...]
        gn = gn_ref[...]
        on = jnp.concatenate(
            [_rms(o[:, h * GLA_DV:(h + 1) * GLA_DV], gn) for h in range(GLA_HEADS)], axis=1)
        g = g_ref[...].astype(F32)
        gl_ref[...] = (on * (g * _sigmoid(g))).astype(BF16)

    bb = jnp.dot(gl_ref[...], wg_ref[j], preferred_element_type=F32)
    aa = jnp.dot(att_ref[...], wa_ref[j], preferred_element_type=F32)
    ga = _sigmoid(gla_ref[...].astype(F32) + bma_ref[...])
    gb = _sigmoid(glb_ref[...].astype(F32) + bmb_ref[...])
    o_ref[...] = (ga * aa + gb * bb).astype(BF16)


def _merge(att, o_f, o_b, proj, w_attn, w_gla, gla_norm, b_merge):
    n = att.shape[0]
    tm = 512
    nb, _, tn = w_attn.shape
    row = lambda w, c=0: pl.BlockSpec((tm, w), lambda i, j: (i, c))
    return pl.pallas_call(
        _merge_kernel,
        out_shape=jax.ShapeDtypeStruct((n, D_MODEL), BF16),
        grid=(n // tm, nb),
        in_specs=[
            row(ATTN_WIDTH), row(GLA_V_WIDTH), row(GLA_V_WIDTH),
            row(GLA_V_WIDTH, COL_GG // GLA_V_WIDTH),
            pl.BlockSpec((tm, tn), lambda i, j: (i, j)),
            pl.BlockSpec((tm, tn), lambda i, j: (i, nb + j)),
            pl.BlockSpec((nb, ATTN_WIDTH, tn), lambda i, j: (0, 0, 0)),
            pl.BlockSpec((nb, GLA_V_WIDTH, tn), lambda i, j: (0, 0, 0)),
            pl.BlockSpec((1, GLA_DV), lambda i, j: (0, 0)),
            pl.BlockSpec((1, tn), lambda i, j: (0, j)),
            pl.BlockSpec((1, tn), lambda i, j: (0, nb + j)),
        ],
        out_specs=pl.BlockSpec((tm, tn), lambda i, j: (i, j)),
        scratch_shapes=[pltpu.VMEM((tm, GLA_V_WIDTH), BF16)],
        compiler_params=_params(("parallel", "arbitrary")),
        name="merge",
    )(att, o_f, o_b, proj, proj, proj, w_attn, w_gla, gla_norm, b_merge, b_merge)


def _outproj_kernel(x_ref, m_ref, w_ref, o_ref):
    o_ref[...] = x_ref[...] + jnp.dot(m_ref[...], w_ref[...], preferred_element_type=F32)


def _outproj(x, mixed, w_out):
    n = x.shape[0]
    tm = 512
    row = pl.BlockSpec((tm, D_MODEL), lambda i: (i, 0))
    return pl.pallas_call(
        _outproj_kernel,
        out_shape=jax.ShapeDtypeStruct((n, D_MODEL), F32),
        grid=(n // tm,),
        in_specs=[row, row, pl.BlockSpec((D_MODEL, D_MODEL), lambda i: (0, 0))],
        out_specs=row,
        compiler_params=_params(("parallel",)),
        name="out_proj",
    )(x, mixed, w_out)


def _mlp_kernel(h_ref, g_ref, wu_ref, wd_ref, gf_ref, o_ref, hn_ref, *, final_norm):
    j = pl.program_id(1)

    @pl.when(j == 0)
    def _():
        h = h_ref[...]
        hn_ref[...] = _rms(h, g_ref[...]).astype(BF16)
        o_ref[...] = h

    u = jnp.maximum(jnp.dot(hn_ref[...], wu_ref[...], preferred_element_type=F32), 0.0)
    o_ref[...] += jnp.dot((u * u).astype(BF16), wd_ref[...], preferred_element_type=F32)

    if final_norm:
        @pl.when(j == pl.num_programs(1) - 1)
        def _():
            o_ref[...] = _rms(o_ref[...], gf_ref[...])


def _mlp(h, gain, w_up, w_down, gain_final, final_norm):
    n = h.shape[0]
    tm = 512
    tf = 1024
    row = pl.BlockSpec((tm, D_MODEL), lambda i, j: (i, 0))
    vec = pl.BlockSpec((1, D_MODEL), lambda i, j: (0, 0))
    return pl.pallas_call(
        functools.partial(_mlp_kernel, final_norm=final_norm),
        out_shape=jax.ShapeDtypeStruct((n, D_MODEL), F32),
        grid=(n // tm, D_FF // tf),
        in_specs=[
            row, vec,
            pl.BlockSpec((D_MODEL, tf), lambda i, j: (0, j)),
            pl.BlockSpec((tf, D_MODEL), lambda i, j: (j, 0)),
            vec,
        ],
        out_specs=row,
        scratch_shapes=[pltpu.VMEM((tm, D_MODEL), BF16)],
        compiler_params=_params(("parallel", "arbitrary")),
        name="mlp",
    )(h, gain, w_up, w_down, gain_final)


def _column_blocks(w):
    k, n = w.shape
    return w.reshape(k, n // MERGE_TN, MERGE_TN).transpose(1, 0, 2)


def _prep_layer(w_in, w_gate_up_fwd, b_gate_fwd, w_gate_up_bwd, b_gate_bwd, w_attn_proj,
                w_gla_proj, w_out, w_up, w_down):
    offs = [0]
    for wdt in (ATTN_WIDTH, KV_WIDTH, KV_WIDTH, GLA_K_WIDTH, GLA_K_WIDTH, GLA_V_WIDTH,
                GLA_V_WIDTH, GLA_GATE_RANK, GLA_GATE_RANK, N_BRANCHES * D_MODEL):
        offs.append(offs[-1] + wdt)
    w_in = w_in.astype(BF16)
    q_a, k_a, v_a, q_g, k_g, v_g, g_g, lr_f, lr_b, gate = [
        w_in[:, offs[i]:offs[i + 1]] for i in range(10)]
    pad = jnp.zeros((D_MODEL, PROJ_W - COL_LR - 2 * GLA_GATE_RANK), BF16)
    w_proj = jnp.concatenate([gate, q_a, v_g, g_g, k_a, v_a, q_g, k_g, lr_f, lr_b, pad], axis=1)
    zf = jnp.zeros((LANES - GLA_GATE_RANK, GLA_K_WIDTH), F32)
    zb = jnp.zeros((LANES - 2 * GLA_GATE_RANK, GLA_K_WIDTH), F32)
    w_f = jnp.concatenate([w_gate_up_fwd, zf], axis=0).astype(BF16)
    w_b = jnp.concatenate([jnp.zeros((GLA_GATE_RANK, GLA_K_WIDTH), F32), w_gate_up_bwd, zb],
                          axis=0).astype(BF16)
    return dict(
        w_proj=w_proj, w_f=w_f, w_b=w_b,
        b_f=b_gate_fwd.reshape(1, -1), b_b=b_gate_bwd.reshape(1, -1),
        w_attn=_column_blocks(w_attn_proj.astype(BF16)), w_gla=_column_blocks(w_gla_proj.astype(BF16)),
        w_out=w_out.astype(BF16), w_up=w_up.astype(BF16), w_down=w_down.astype(BF16))


def _layer(x, b, t, lw, norm_mix, q_norm, k_norm, gla_norm, b_merge, norm_mlp, norm_final,
           final_norm):
    proj = _inproj(x, norm_mix.reshape(1, -1), lw["w_proj"])
    q, k, vt = _prep(proj, _rope_tables(t), q_norm.reshape(1, -1), k_norm.reshape(1, -1), b, t)
    att = _attention(q, k, vt, _scores_bounded(q_norm, k_norm), b, t)
    o_f, o_b = _gla(proj, lw["w_f"], lw["b_f"], lw["w_b"], lw["b_b"], b, t)
    mixed = _merge(att, o_f, o_b, proj, lw["w_attn"], lw["w_gla"], gla_norm.reshape(1, -1),
                   b_merge.reshape(1, -1))
    h = _outproj(x, mixed, lw["w_out"])
    return _mlp(h, norm_mlp.reshape(1, -1), lw["w_up"], lw["w_down"], norm_final.reshape(1, -1),
                final_norm)


def kernel(x_prompt, x_sample, norm_mix, w_in, q_norm, k_norm, w_gate_up_fwd, b_gate_fwd,
           w_gate_up_bwd, b_gate_bwd, gla_norm, w_attn_proj, w_gla_proj, b_merge, w_out,
           norm_mlp, w_up, w_down, norm_final):
    layers = [
        _prep_layer(w_in[l], w_gate_up_fwd[l], b_gate_fwd[l], w_gate_up_bwd[l], b_gate_bwd[l],
                    w_attn_proj[l], w_gla_proj[l], w_out[l], w_up[l], w_down[l])
        for l in range(DEPTH)]

    def trunk(x):
        b, t, d = x.shape
        y = x.reshape(b * t, d)
        for l in range(DEPTH):
            y = _layer(y, b, t, layers[l], norm_mix[l], q_norm[l], k_norm[l], gla_norm[l],
                       b_merge[l], norm_mlp[l], norm_final, final_norm=(l == DEPTH - 1))
        return y.reshape(b, t, d)

    return trunk(x_prompt), trunk(x_sample)
```

```python
import functools

import jax
import jax.numpy as jnp
from jax import lax
from jax.experimental import pallas as pl
from jax.experimental.pallas import tpu as pltpu

F32 = jnp.float32
BF16 = jnp.bfloat16

D_MODEL = 2048
DEPTH = 1
GRID_W = 64
HEAD_DIM = 128
N_Q_HEADS = 8
N_KV_HEADS = 2
Q_GROUP = N_Q_HEADS // N_KV_HEADS
ATTN_WIDTH = N_Q_HEADS * HEAD_DIM
KV_WIDTH = N_KV_HEADS * HEAD_DIM
ROPE_THETA = 10000.0
GLA_HEADS = 4
GLA_DK = 128
GLA_DV = 256
GLA_K_WIDTH = GLA_HEADS * GLA_DK
GLA_V_WIDTH = GLA_HEADS * GLA_DV
GLA_GATE_RANK = 16
GLA_GATE_NORMALIZER = 16.0
GLA_CHUNK = 64
N_BRANCHES = 2
D_FF = 4 * D_MODEL
NORM_EPS = 1e-6

LANES = 128
SUBLANES = 8
VMEM_LIMIT = 48 * 1024 * 1024

COL_GATE = 0
COL_QA = COL_GATE + N_BRANCHES * D_MODEL
COL_VG = COL_QA + ATTN_WIDTH
COL_GG = COL_VG + GLA_V_WIDTH
COL_KVA = COL_GG + GLA_V_WIDTH
COL_QG = COL_KVA + 2 * KV_WIDTH
COL_KG = COL_QG + GLA_K_WIDTH
COL_LR = COL_KG + GLA_K_WIDTH
PROJ_TN = 1280
PROJ_W = 7 * PROJ_TN

GLA_TB = 256
MERGE_TN = D_MODEL // 2
ATT_TQ = 256
ATT_TK = 512
ATT_KV_GROUP = 8
LOG2_E = 1.4426950408889634
ATT_BOUNDED_MAX_LOG2 = 60.0


def _params(sem):
    return pltpu.CompilerParams(dimension_semantics=sem, vmem_limit_bytes=VMEM_LIMIT)


def _sigmoid(x):
    return 1.0 / (1.0 + jnp.exp(-x))


def _rms(x, gain):
    ms = jnp.mean(x * x, axis=-1, keepdims=True)
    return x * lax.rsqrt(ms + NORM_EPS) * gain


def _inproj_kernel(x_ref, g_ref, w_ref, o_ref, xn_ref):
    @pl.when(pl.program_id(1) == 0)
    def _():
        xn_ref[...] = _rms(x_ref[...], g_ref[...]).astype(BF16)

    o_ref[...] = jnp.dot(xn_ref[...], w_ref[...], preferred_element_type=F32).astype(o_ref.dtype)


def _inproj(x, gain, w):
    n = x.shape[0]
    tm = 1024
    return pl.pallas_call(
        _inproj_kernel,
        out_shape=jax.ShapeDtypeStruct((n, PROJ_W), BF16),
        grid=(n // tm, PROJ_W // PROJ_TN),
        in_specs=[
            pl.BlockSpec((tm, D_MODEL), lambda i, j: (i, 0)),
            pl.BlockSpec((1, D_MODEL), lambda i, j: (0, 0)),
            pl.BlockSpec((D_MODEL, PROJ_TN), lambda i, j: (0, j)),
        ],
        out_specs=pl.BlockSpec((tm, PROJ_TN), lambda i, j: (i, j)),
        scratch_shapes=[pltpu.VMEM((tm, D_MODEL), BF16)],
        compiler_params=_params(("parallel", "arbitrary")),
        name="in_proj",
    )(x, gain, w)


def _prep_kernel(q_ref, kv_ref, rt_ref, ct_ref, qg_ref, kg_ref, qo_ref, ko_ref, vt_ref):
    rows_per_tile = rt_ref.shape[1]

    def table(c):
        rt = rt_ref[c]
        by_row = jnp.concatenate(
            [jnp.broadcast_to(rt[g:g + 1], (GRID_W, HEAD_DIM)) for g in range(rows_per_tile)],
            axis=0)
        return by_row + jnp.tile(ct_ref[c], (rows_per_tile, 1))

    cos, sa, sb = table(0), table(1), table(2)

    def norm_rope(x, gain):
        y = _rms(x.astype(F32), gain)
        return y * cos + pltpu.roll(y, 96, 1) * sa + pltpu.roll(y, 32, 1) * sb

    qg = qg_ref[...]
    kg = kg_ref[...]
    for h in range(N_Q_HEADS):
        sl = slice(h * HEAD_DIM, (h + 1) * HEAD_DIM)
        qo_ref[:, sl] = (norm_rope(q_ref[:, sl], qg) * (LOG2_E * HEAD_DIM ** -0.5)).astype(BF16)
    for h in range(N_KV_HEADS):
        sl = slice(h * HEAD_DIM, (h + 1) * HEAD_DIM)
        ko_ref[:, sl] = norm_rope(kv_ref[:, sl], kg).astype(BF16)
        v = kv_ref[:, KV_WIDTH + h * HEAD_DIM:KV_WIDTH + (h + 1) * HEAD_DIM]
        vt_ref[0, h, 0] = v.astype(F32).T.astype(BF16)


def _prep(proj, tables, q_gain, k_gain, b, t):
    n = b * t
    tm = ATT_TK
    nt = t // tm
    row_tab, col_tab = tables
    rows_per_tile = tm // GRID_W
    row_spec = pl.BlockSpec((3, rows_per_tile, HEAD_DIM), lambda i: (0, i % nt, 0))
    col_spec = pl.BlockSpec((3, GRID_W, HEAD_DIM), lambda i: (0, 0, 0))
    gain_spec = pl.BlockSpec((1, HEAD_DIM), lambda i: (0, 0))
    return pl.pallas_call(
        _prep_kernel,
        out_shape=(
            jax.ShapeDtypeStruct((n, ATTN_WIDTH), BF16),
            jax.ShapeDtypeStruct((n, KV_WIDTH), BF16),
            jax.ShapeDtypeStruct((b, N_KV_HEADS, nt, HEAD_DIM, tm), BF16),
        ),
        grid=(n // tm,),
        in_specs=[
            pl.BlockSpec((tm, ATTN_WIDTH), lambda i: (i, COL_QA // ATTN_WIDTH)),
            pl.BlockSpec((tm, 2 * KV_WIDTH), lambda i: (i, COL_KVA // (2 * KV_WIDTH))),
            row_spec, col_spec, gain_spec, gain_spec,
        ],
        out_specs=(
            pl.BlockSpec((tm, ATTN_WIDTH), lambda i: (i, 0)),
            pl.BlockSpec((tm, KV_WIDTH), lambda i: (i, 0)),
            pl.BlockSpec((1, N_KV_HEADS, 1, HEAD_DIM, tm), lambda i: (i // nt, 0, i % nt, 0, 0)),
        ),
        compiler_params=_params(("parallel",)),
        name="qkv_prep",
    )(proj, proj, row_tab, col_tab, q_gain, k_gain)


def _rope_tables(t):
    n_rows = t // GRID_W
    sec = HEAD_DIM // 2
    inv_freq = ROPE_THETA ** (-jnp.arange(0, sec, 2, dtype=F32) / sec)
    def planes(ang):
        c, s, z = jnp.cos(ang), jnp.sin(ang), jnp.zeros_like(ang)
        return jnp.stack([jnp.concatenate([c, c], 1), jnp.concatenate([-s, z], 1),
                          jnp.concatenate([z, s], 1)])

    row = planes(jnp.arange(n_rows, dtype=F32)[:, None] * inv_freq[None, :])
    col = planes(jnp.arange(GRID_W, dtype=F32)[:, None] * inv_freq[None, :])
    row_tab = jnp.concatenate([row, jnp.zeros_like(row)], axis=2)
    col_tab = jnp.concatenate([jnp.zeros_like(col), col], axis=2)
    return row_tab, col_tab


def _attn_kernel(bounded_ref, q_ref, k_ref, vt_ref, o_ref, acc_ref, l_ref, m_ref, *, nk):
    tq = q_ref.shape[0]
    tk = vt_ref.shape[-1]
    cols = Q_GROUP * tq
    sub = l_ref.shape[0]

    qt = jnp.concatenate(
        [q_ref[:, g * HEAD_DIM:(g + 1) * HEAD_DIM].astype(F32).T.astype(BF16)
         for g in range(Q_GROUP)], axis=1)

    def kv_tile(j):
        return k_ref[pl.ds(pl.multiple_of(j * tk, tk), tk), :], vt_ref[0, 0, j]

    def sublane_partial_sum(p):
        return jnp.sum(p.reshape(tk // sub, sub, cols), axis=0)

    acc_ref[...] = jnp.zeros_like(acc_ref)
    l_ref[...] = jnp.zeros_like(l_ref)

    @pl.when(bounded_ref[0] != 0)
    def _():
        group = min(ATT_KV_GROUP, nk)

        def body(jg, carry):
            l_sum = pv_sum = None
            for u in range(group):
                k, vt = kv_tile(jg * group + u)
                p = jnp.exp2(jnp.dot(k, qt, preferred_element_type=F32))
                l_u = sublane_partial_sum(p)
                pv_u = jnp.dot(vt, p.astype(BF16), preferred_element_type=F32)
                l_sum = l_u if l_sum is None else l_sum + l_u
                pv_sum = pv_u if pv_sum is None else pv_sum + pv_u
            l_ref[...] += l_sum
            acc_ref[...] += pv_sum
            return carry

        lax.fori_loop(0, nk // group, body, 0)

    @pl.when(bounded_ref[0] == 0)
    def _():
        m_ref[...] = jnp.full_like(m_ref, -jnp.inf)

        def body(j, carry):
            k, vt = kv_tile(j)
            s = jnp.dot(k, qt, preferred_element_type=F32)
            m_old = m_ref[...]
            m_new = jnp.maximum(m_old, jnp.max(s, axis=0, keepdims=True))
            alpha = jnp.exp2(m_old - m_new)
            p = jnp.exp2(s - m_new)
            m_ref[...] = m_new
            l_ref[...] = alpha * l_ref[...] + sublane_partial_sum(p)
            acc_ref[...] = alpha * acc_ref[...] + jnp.dot(vt, p.astype(BF16),
                                                          preferred_element_type=F32)
            return carry

        lax.fori_loop(0, nk, body, 0)

    out_t = acc_ref[...] * (1.0 / jnp.sum(l_ref[...], axis=0, keepdims=True))
    for g in range(Q_GROUP):
        o_ref[:, g * HEAD_DIM:(g + 1) * HEAD_DIM] = out_t[:, g * tq:(g + 1) * tq].T.astype(BF16)


def _scores_bounded(q_gain, k_gain):
    bound = (1.02 * LOG2_E * HEAD_DIM ** 0.5) * jnp.max(jnp.abs(q_gain)) * jnp.max(jnp.abs(k_gain))
    return (bound <= ATT_BOUNDED_MAX_LOG2).astype(jnp.int32).reshape(1)


def _attention(q, k, vt, bounded, b, t):
    n = b * t
    tq = ATT_TQ
    nq = t // tq
    nk = t // ATT_TK
    gw = Q_GROUP * HEAD_DIM
    return pl.pallas_call(
        functools.partial(_attn_kernel, nk=nk),
        out_shape=jax.ShapeDtypeStruct((n, ATTN_WIDTH), BF16),
        grid=(b, N_KV_HEADS, nq),
        in_specs=[
            pl.BlockSpec(memory_space=pltpu.SMEM),
            pl.BlockSpec((tq, gw), lambda bi, h, i: (bi * nq + i, h)),
            pl.BlockSpec((t, HEAD_DIM), lambda bi, h, i: (bi, h)),
            pl.BlockSpec((1, 1, nk, HEAD_DIM, ATT_TK), lambda bi, h, i: (bi, h, 0, 0, 0)),
        ],
        out_specs=pl.BlockSpec((tq, gw), lambda bi, h, i: (bi * nq + i, h)),
        scratch_shapes=[
            pltpu.VMEM((HEAD_DIM, Q_GROUP * tq), F32),
            pltpu.VMEM((SUBLANES, Q_GROUP * tq), F32),
            pltpu.VMEM((1, Q_GROUP * tq), F32),
        ],
        compiler_params=_params(("parallel", "parallel", "arbitrary")),
        name="attention",
    )(bounded, q, k, vt)


def _gla_dir(q_ref, k_ref, v_ref, lr_ref, w_ref, b_ref, tri_ref, mask_ref, o_ref, s_ref, cs_ref,
             reverse):
    c_sz = GLA_CHUNK
    n_chunks = GLA_TB // c_sz
    z = jnp.dot(lr_ref[...], w_ref[...], preferred_element_type=F32) + b_ref[...]
    softplus2 = jnp.log2(1.0 + jnp.exp2(jnp.abs(z) * (-LOG2_E)))
    la = (jnp.minimum(z, 0.0) * (LOG2_E / GLA_GATE_NORMALIZER)
          - softplus2 * (1.0 / GLA_GATE_NORMALIZER))
    hi = la.astype(BF16)
    lo = (la - hi.astype(F32)).astype(BF16)
    tri = tri_ref[...]
    cs = (jnp.dot(tri, hi, preferred_element_type=F32)
          + jnp.dot(tri, lo, preferred_element_type=F32))
    for h in range(GLA_HEADS):
        cs_ref[h] = cs[:, h * GLA_DK:(h + 1) * GLA_DK]
    half = c_sz // 2
    edge_start = 0 if reverse else half - 1
    edge_row = [2 * c if reverse else 2 * c + 1 for c in range(n_chunks)]

    qe = (q_ref[...].astype(F32) * (GLA_DK ** -0.5)) * jnp.exp2(cs)
    ke = k_ref[...].astype(F32) * jnp.exp2(-cs)
    qe_b = qe.astype(BF16)
    ke_b = ke.astype(BF16)
    v_b = v_ref[...]
    mask = mask_ref[...] > 0.5
    chunk_of_lane = lax.broadcasted_iota(jnp.int32, (GLA_DK, GLA_TB), 1) // c_sz
    order = range(n_chunks - 1, -1, -1) if reverse else range(n_chunks)

    def head(h):
        ks = slice(h * GLA_DK, (h + 1) * GLA_DK)
        vs = slice(h * GLA_DV, (h + 1) * GLA_DV)
        a = lax.dot_general(qe_b[:, ks], ke_b[:, ks], (((1,), (1,)), ((), ())),
                            preferred_element_type=F32)
        a = jnp.where(mask, a, 0.0).astype(BF16)
        edge = jnp.exp2(cs_ref[h, pl.ds(edge_start, 2 * n_chunks, stride=half), :])
        decay_t = jnp.tile(edge, (GLA_DK // (2 * n_chunks), 1)).T
        kd = ke[:, ks] * jnp.concatenate(
            [jnp.broadcast_to(edge[r:r + 1], (c_sz, GLA_DK)) for r in edge_row], axis=0)
        kd_t = kd.T
        lhs = jnp.concatenate(
            [a] + [jnp.where(chunk_of_lane == c, kd_t, 0.0).astype(BF16) for c in range(n_chunks)],
            axis=0)
        prod = jnp.dot(lhs, v_b[:, vs], preferred_element_type=F32)
        s = s_ref[h]
        for c in order:
            rs = slice(c * c_sz, (c + 1) * c_sz)
            r = edge_row[c]
            o_inter = jnp.dot(qe_b[rs, ks], s.astype(BF16), preferred_element_type=F32)
            o_ref[rs, vs] = prod[rs] + o_inter
            inc = prod[GLA_TB + c * GLA_DK:GLA_TB + (c + 1) * GLA_DK]
            s = decay_t[:, r:r + 1] * s + inc
        s_ref[h] = s

    for h in range(GLA_HEADS):
        head(h)


def _gla_kernel(qf_ref, kf_ref, vf_ref, lrf_ref, qb_ref, kb_ref, vb_ref, lrb_ref,
                wf_ref, bf_ref, wb_ref, bb_ref, trif_ref, trib_ref, maskf_ref, maskb_ref,
                of_ref, ob_ref, sf_ref, sb_ref, csf_ref, csb_ref):
    @pl.when(pl.program_id(1) == 0)
    def _():
        sf_ref[...] = jnp.zeros_like(sf_ref)
        sb_ref[...] = jnp.zeros_like(sb_ref)

    _gla_dir(qf_ref, kf_ref, vf_ref, lrf_ref, wf_ref, bf_ref, trif_ref, maskf_ref, of_ref,
             sf_ref, csf_ref, reverse=False)
    _gla_dir(qb_ref, kb_ref, vb_ref, lrb_ref, wb_ref, bb_ref, trib_ref, maskb_ref, ob_ref,
             sb_ref, csb_ref, reverse=True)


def _gla_consts():
    i = jnp.arange(GLA_TB)[:, None]
    j = jnp.arange(GLA_TB)[None, :]
    same = (i // GLA_CHUNK) == (j // GLA_CHUNK)
    tri_f = (same & (j <= i)).astype(F32)
    tri_b = (same & (j >= i)).astype(F32)
    mask_b = (same & (j > i)).astype(F32)
    return tri_f, tri_b, mask_b


def _gla(proj, w_f, b_f, w_b, b_b, b, t):
    n = b * t
    tb = GLA_TB
    nt = t // tb
    tri_f, tri_b, mask_b = _gla_consts()

    def fwd(width, col):
        return pl.BlockSpec((tb, width), lambda bi, i: (bi * nt + i, col // width))

    def bwd(width, col):
        return pl.BlockSpec((tb, width), lambda bi, i: (bi * nt + nt - 1 - i, col // width))

    def const(shape):
        return pl.BlockSpec(shape, lambda bi, i: (0, 0))

    pieces = ((GLA_K_WIDTH, COL_QG), (GLA_K_WIDTH, COL_KG), (GLA_V_WIDTH, COL_VG), (LANES, COL_LR))
    out_sd = jax.ShapeDtypeStruct((n, GLA_V_WIDTH), F32)
    return pl.pallas_call(
        _gla_kernel,
        out_shape=(out_sd, out_sd),
        grid=(b, nt),
        in_specs=[fwd(*p) for p in pieces] + [bwd(*p) for p in pieces] + [
            const((LANES, GLA_K_WIDTH)), const((1, GLA_K_WIDTH)),
            const((LANES, GLA_K_WIDTH)), const((1, GLA_K_WIDTH)),
            const((tb, tb)), const((tb, tb)), const((tb, tb)), const((tb, tb)),
        ],
        out_specs=(
            pl.BlockSpec((tb, GLA_V_WIDTH), lambda bi, i: (bi * nt + i, 0)),
            pl.BlockSpec((tb, GLA_V_WIDTH), lambda bi, i: (bi * nt + nt - 1 - i, 0)),
        ),
        scratch_shapes=[
            pltpu.VMEM((GLA_HEADS, GLA_DK, GLA_DV), F32),
            pltpu.VMEM((GLA_HEADS, GLA_DK, GLA_DV), F32),
            pltpu.VMEM((GLA_HEADS, tb, GLA_DK), F32),
            pltpu.VMEM((GLA_HEADS, tb, GLA_DK), F32),
        ],
        compiler_params=_params(("parallel", "arbitrary")),
        name="gla",
    )(*([proj] * 8), w_f, b_f, w_b, b_b, tri_f.astype(BF16), tri_b.astype(BF16), tri_f, mask_b)


def _merge_kernel(att_ref, of_ref, ob_ref, g_ref, gla_ref, glb_ref, wa_ref, wg_ref, gn_ref,
                  bma_ref, bmb_ref, o_ref):
    o = of_ref[...] + ob_ref[...]
    gn = gn_ref[...]
    on = jnp.concatenate(
        [_rms(o[:, h * GLA_DV:(h + 1) * GLA_DV], gn) for h in range(GLA_HEADS)], axis=1)
    g = g_ref[...].astype(F32)
    gl = (on * (g * _sigmoid(g))).astype(BF16)
    att = att_ref[...]
    for c in range(D_MODEL // MERGE_TN):
        cs = slice(c * MERGE_TN, (c + 1) * MERGE_TN)
        bb = jnp.dot(gl, wg_ref[:, cs], preferred_element_type=F32)
        aa = jnp.dot(att, wa_ref[:, cs], preferred_element_type=F32)
        ga = _sigmoid(gla_ref[:, cs].astype(F32) + bma_ref[:, cs])
        gb = _sigmoid(glb_ref[:, cs].astype(F32) + bmb_ref[:, cs])
        o_ref[:, cs] = (ga * aa + gb * bb).astype(BF16)


def _merge(att, o_f, o_b, proj, w_attn, w_gla, gla_norm, b_merge):
    n = att.shape[0]
    tm = 512
    row = lambda w, c=0: pl.BlockSpec((tm, w), lambda i: (i, c))
    const = lambda r, w, c=0: pl.BlockSpec((r, w), lambda i: (0, c))
    resident = lambda r, w: pl.BlockSpec((r, w), lambda i: (0, 0), pipeline_mode=pl.Buffered(1))
    return pl.pallas_call(
        _merge_kernel,
        out_shape=jax.ShapeDtypeStruct((n, D_MODEL), BF16),
        grid=(n // tm,),
        in_specs=[
            row(ATTN_WIDTH), row(GLA_V_WIDTH), row(GLA_V_WIDTH),
            row(GLA_V_WIDTH, COL_GG // GLA_V_WIDTH),
            row(D_MODEL, 0), row(D_MODEL, 1),
            resident(ATTN_WIDTH, D_MODEL), resident(GLA_V_WIDTH, D_MODEL), const(1, GLA_DV),
            const(1, D_MODEL, 0), const(1, D_MODEL, 1),
        ],
        out_specs=row(D_MODEL),
        compiler_params=_params(("parallel",)),
        name="merge",
    )(att, o_f, o_b, proj, proj, proj, w_attn, w_gla, gla_norm, b_merge, b_merge)


def _outproj_kernel(x_ref, m_ref, w_ref, o_ref):
    o_ref[...] = x_ref[...] + jnp.dot(m_ref[...], w_ref[...], preferred_element_type=F32)


def _outproj(x, mixed, w_out):
    n = x.shape[0]
    tm = 512
    row = pl.BlockSpec((tm, D_MODEL), lambda i: (i, 0))
    return pl.pallas_call(
        _outproj_kernel,
        out_shape=jax.ShapeDtypeStruct((n, D_MODEL), F32),
        grid=(n // tm,),
        in_specs=[row, row, pl.BlockSpec((D_MODEL, D_MODEL), lambda i: (0, 0))],
        out_specs=row,
        compiler_params=_params(("parallel",)),
        name="out_proj",
    )(x, mixed, w_out)


def _mlp_kernel(h_ref, g_ref, wu_ref, wd_ref, gf_ref, o_ref, hn_ref, *, final_norm):
    j = pl.program_id(1)

    @pl.when(j == 0)
    def _():
        h = h_ref[...]
        hn_ref[...] = _rms(h, g_ref[...]).astype(BF16)
        o_ref[...] = h

    u = jnp.maximum(jnp.dot(hn_ref[...], wu_ref[...], preferred_element_type=F32), 0.0)
    o_ref[...] += jnp.dot((u * u).astype(BF16), wd_ref[...], preferred_element_type=F32)

    if final_norm:
        @pl.when(j == pl.num_programs(1) - 1)
        def _():
            o_ref[...] = _rms(o_ref[...], gf_ref[...])


def _mlp(h, gain, w_up, w_down, gain_final, final_norm):
    n = h.shape[0]
    tm = 512
    tf = 1024
    row = pl.BlockSpec((tm, D_MODEL), lambda i, j: (i, 0))
    vec = pl.BlockSpec((1, D_MODEL), lambda i, j: (0, 0))
    return pl.pallas_call(
        functools.partial(_mlp_kernel, final_norm=final_norm),
        out_shape=jax.ShapeDtypeStruct((n, D_MODEL), F32),
        grid=(n // tm, D_FF // tf),
        in_specs=[
            row, vec,
            pl.BlockSpec((D_MODEL, tf), lambda i, j: (0, j)),
            pl.BlockSpec((tf, D_MODEL), lambda i, j: (j, 0)),
            vec,
        ],
        out_specs=row,
        scratch_shapes=[pltpu.VMEM((tm, D_MODEL), BF16)],
        compiler_params=_params(("parallel", "arbitrary")),
        name="mlp",
    )(h, gain, w_up, w_down, gain_final)


_SRC_WIDTHS = (ATTN_WIDTH, KV_WIDTH, KV_WIDTH, GLA_K_WIDTH, GLA_K_WIDTH, GLA_V_WIDTH, GLA_V_WIDTH,
               GLA_GATE_RANK, GLA_GATE_RANK, N_BRANCHES * D_MODEL)
_SRC_OFF = tuple(sum(_SRC_WIDTHS[:i]) for i in range(len(_SRC_WIDTHS) + 1))
D_IN_PROJ = _SRC_OFF[-1]
_PROJ_MOVES = ((COL_QA, 0), (COL_KVA, 1), (COL_KVA + KV_WIDTH, 2), (COL_QG, 3), (COL_KG, 4),
               (COL_VG, 5), (COL_GG, 6))


def _wproj_kernel(wt_ref, o_ref):
    def put(dst, src, width):
        o_ref[:, dst:dst + width] = wt_ref[src:src + width, :].T.astype(BF16)

    put(COL_GATE, _SRC_OFF[9], N_BRANCHES * D_MODEL)
    for dst, piece in _PROJ_MOVES:
        put(dst, _SRC_OFF[piece], _SRC_WIDTHS[piece])
    lr = wt_ref[_SRC_OFF[7]:_SRC_OFF[7] + LANES, :].T
    lane = lax.broadcasted_iota(jnp.int32, lr.shape, 1)
    o_ref[:, COL_LR:COL_LR + LANES] = jnp.where(lane < 2 * GLA_GATE_RANK, lr, 0.0).astype(BF16)
    o_ref[:, COL_LR + LANES:] = jnp.zeros((o_ref.shape[0], PROJ_W - COL_LR - LANES), BF16)


def _wproj(w_in):
    wt = jnp.swapaxes(w_in, 0, 1)
    return pl.pallas_call(
        _wproj_kernel,
        out_shape=jax.ShapeDtypeStruct((D_MODEL, PROJ_W), BF16),
        grid=(D_MODEL // LANES,),
        in_specs=[pl.BlockSpec((D_IN_PROJ, LANES), lambda i: (0, i))],
        out_specs=pl.BlockSpec((LANES, PROJ_W), lambda i: (i, 0)),
        compiler_params=_params(("parallel",)),
        name="w_proj_layout",
    )(wt)


def _prep_layer(w_in, w_gate_up_fwd, b_gate_fwd, w_gate_up_bwd, b_gate_bwd, w_attn_proj,
                w_gla_proj, w_out, w_up, w_down):
    w_proj = _wproj(w_in)
    zf = jnp.zeros((LANES - GLA_GATE_RANK, GLA_K_WIDTH), F32)
    zb = jnp.zeros((LANES - 2 * GLA_GATE_RANK, GLA_K_WIDTH), F32)
    w_f = jnp.concatenate([w_gate_up_fwd, zf], axis=0).astype(BF16)
    w_b = jnp.concatenate([jnp.zeros((GLA_GATE_RANK, GLA_K_WIDTH), F32), w_gate_up_bwd, zb],
                          axis=0).astype(BF16)
    return dict(
        w_proj=w_proj, w_f=w_f, w_b=w_b,
        b_f=b_gate_fwd.reshape(1, -1), b_b=b_gate_bwd.reshape(1, -1),
        w_attn=w_attn_proj.astype(BF16), w_gla=w_gla_proj.astype(BF16),
        w_out=w_out.astype(BF16), w_up=w_up.astype(BF16), w_down=w_down.astype(BF16))


def _layer(x, b, t, lw, norm_mix, q_norm, k_norm, gla_norm, b_merge, norm_mlp, norm_final,
           final_norm):
    proj = _inproj(x, norm_mix.reshape(1, -1), lw["w_proj"])
    q, k, vt = _prep(proj, _rope_tables(t), q_norm.reshape(1, -1), k_norm.reshape(1, -1), b, t)
    att = _attention(q, k, vt, _scores_bounded(q_norm, k_norm), b, t)
    o_f, o_b = _gla(proj, lw["w_f"], lw["b_f"], lw["w_b"], lw["b_b"], b, t)
    mixed = _merge(att, o_f, o_b, proj, lw["w_attn"], lw["w_gla"], gla_norm.reshape(1, -1),
                   b_merge.reshape(1, -1))
    h = _outproj(x, mixed, lw["w_out"])
    return _mlp(h, norm_mlp.reshape(1, -1), lw["w_up"], lw["w_down"], norm_final.reshape(1, -1),
                final_norm)


def kernel(x_prompt, x_sample, norm_mix, w_in, q_norm, k_norm, w_gate_up_fwd, b_gate_fwd,
           w_gate_up_bwd, b_gate_bwd, gla_norm, w_attn_proj, w_gla_proj, b_merge, w_out,
           norm_mlp, w_up, w_down, norm_final):
    layers = [
        _prep_layer(w_in[l], w_gate_up_fwd[l], b_gate_fwd[l], w_gate_up_bwd[l], b_gate_bwd[l],
                    w_attn_proj[l], w_gla_proj[l], w_out[l], w_up[l], w_down[l])
        for l in range(DEPTH)]

    def trunk(x):
        b, t, d = x.shape
        y = x.reshape(b * t, d)
        for l in range(DEPTH):
            y = _layer(y, b, t, layers[l], norm_mix[l], q_norm[l], k_norm[l], gla_norm[l],
                       b_merge[l], norm_mlp[l], norm_final, final_norm=(l == DEPTH - 1))
        return y.reshape(b, t, d)

    return trunk(x_prompt), trunk(x_sample)
```

```python
import functools
import itertools

import jax
import jax.numpy as jnp
from jax import lax
from jax.experimental import pallas as pl
from jax.experimental.pallas import tpu as pltpu

F32 = jnp.float32
BF16 = jnp.bfloat16

D_MODEL = 2048
DEPTH = 1
GRID_W = 64
HEAD_DIM = 128
N_Q_HEADS = 8
N_KV_HEADS = 2
Q_GROUP = N_Q_HEADS // N_KV_HEADS
ATTN_WIDTH = N_Q_HEADS * HEAD_DIM
KV_WIDTH = N_KV_HEADS * HEAD_DIM
ROPE_THETA = 10000.0
GLA_HEADS = 4
GLA_DK = 128
GLA_DV = 256
GLA_K_WIDTH = GLA_HEADS * GLA_DK
GLA_V_WIDTH = GLA_HEADS * GLA_DV
GLA_GATE_RANK = 16
GLA_GATE_NORMALIZER = 16.0
GLA_CHUNK = 64
N_BRANCHES = 2
D_FF = 4 * D_MODEL
NORM_EPS = 1e-6

LANES = 128
SUBLANES = 8
VMEM_LIMIT = 48 * 1024 * 1024
MLP_VMEM_LIMIT = 58 * 1024 * 1024

COL_GATE = 0
COL_QA = COL_GATE + N_BRANCHES * D_MODEL
COL_VG = COL_QA + ATTN_WIDTH
COL_GG = COL_VG + GLA_V_WIDTH
COL_KVA = COL_GG + GLA_V_WIDTH
COL_QG = COL_KVA + 2 * KV_WIDTH
COL_KG = COL_QG + GLA_K_WIDTH
COL_LR = COL_KG + GLA_K_WIDTH
PROJ_TN = 1280
PROJ_W = 7 * PROJ_TN

GLA_TB = 256
MERGE_TN = D_MODEL // 2
ATT_TQ = 256
ATT_TK = 512
ATT_KV_GROUP = 8
LOG2_E = 1.4426950408889634
ATT_BOUNDED_MAX_LOG2 = 60.0


def _params(sem, vmem_limit=VMEM_LIMIT):
    return pltpu.CompilerParams(dimension_semantics=sem, vmem_limit_bytes=vmem_limit)


def _sigmoid(x):
    return 1.0 / (1.0 + jnp.exp(-x))


def _rms(x, gain):
    ms = jnp.mean(x * x, axis=-1, keepdims=True)
    return x * lax.rsqrt(ms + NORM_EPS) * gain


def _inproj_kernel(x_ref, g_ref, w_ref, o_ref, xn_ref):
    @pl.when(pl.program_id(1) == 0)
    def _():
        xn_ref[...] = _rms(x_ref[...], g_ref[...]).astype(BF16)

    o_ref[...] = jnp.dot(xn_ref[...], w_ref[...], preferred_element_type=F32).astype(o_ref.dtype)


def _inproj(x, gain, w):
    n = x.shape[0]
    tm = 1024
    return pl.pallas_call(
        _inproj_kernel,
        out_shape=jax.ShapeDtypeStruct((n, PROJ_W), BF16),
        grid=(n // tm, PROJ_W // PROJ_TN),
        in_specs=[
            pl.BlockSpec((tm, D_MODEL), lambda i, j: (i, 0)),
            pl.BlockSpec((1, D_MODEL), lambda i, j: (0, 0)),
            pl.BlockSpec((D_MODEL, PROJ_TN), lambda i, j: (0, j)),
        ],
        out_specs=pl.BlockSpec((tm, PROJ_TN), lambda i, j: (i, j)),
        scratch_shapes=[pltpu.VMEM((tm, D_MODEL), BF16)],
        compiler_params=_params(("parallel", "arbitrary")),
        name="in_proj",
    )(x, gain, w)


def _prep_kernel(q_ref, kv_ref, rt_ref, ct_ref, qg_ref, kg_ref, qo_ref, ko_ref, vt_ref):
    rows_per_tile = rt_ref.shape[1]

    def table(c):
        rt = rt_ref[c]
        by_row = jnp.concatenate(
            [jnp.broadcast_to(rt[g:g + 1], (GRID_W, HEAD_DIM)) for g in range(rows_per_tile)],
            axis=0)
        return by_row + jnp.tile(ct_ref[c], (rows_per_tile, 1))

    cos, sa, sb = table(0), table(1), table(2)

    def norm_rope(x, gain):
        y = _rms(x.astype(F32), gain)
        return y * cos + pltpu.roll(y, 96, 1) * sa + pltpu.roll(y, 32, 1) * sb

    qg = qg_ref[...]
    kg = kg_ref[...]
    for h in range(N_Q_HEADS):
        sl = slice(h * HEAD_DIM, (h + 1) * HEAD_DIM)
        qo_ref[:, sl] = (norm_rope(q_ref[:, sl], qg) * (LOG2_E * HEAD_DIM ** -0.5)).astype(BF16)
    for h in range(N_KV_HEADS):
        sl = slice(h * HEAD_DIM, (h + 1) * HEAD_DIM)
        ko_ref[:, sl] = norm_rope(kv_ref[:, sl], kg).astype(BF16)
        v = kv_ref[:, KV_WIDTH + h * HEAD_DIM:KV_WIDTH + (h + 1) * HEAD_DIM]
        vt_ref[0, h, 0] = v.astype(F32).T.astype(BF16)


def _prep(proj, tables, q_gain, k_gain, b, t):
    n = b * t
    tm = ATT_TK
    nt = t // tm
    row_tab, col_tab = tables
    rows_per_tile = tm // GRID_W
    row_spec = pl.BlockSpec((3, rows_per_tile, HEAD_DIM), lambda i: (0, i % nt, 0))
    col_spec = pl.BlockSpec((3, GRID_W, HEAD_DIM), lambda i: (0, 0, 0))
    gain_spec = pl.BlockSpec((1, HEAD_DIM), lambda i: (0, 0))
    return pl.pallas_call(
        _prep_kernel,
        out_shape=(
            jax.ShapeDtypeStruct((n, ATTN_WIDTH), BF16),
            jax.ShapeDtypeStruct((n, KV_WIDTH), BF16),
            jax.ShapeDtypeStruct((b, N_KV_HEADS, nt, HEAD_DIM, tm), BF16),
        ),
        grid=(n // tm,),
        in_specs=[
            pl.BlockSpec((tm, ATTN_WIDTH), lambda i: (i, COL_QA // ATTN_WIDTH)),
            pl.BlockSpec((tm, 2 * KV_WIDTH), lambda i: (i, COL_KVA // (2 * KV_WIDTH))),
            row_spec, col_spec, gain_spec, gain_spec,
        ],
        out_specs=(
            pl.BlockSpec((tm, ATTN_WIDTH), lambda i: (i, 0)),
            pl.BlockSpec((tm, KV_WIDTH), lambda i: (i, 0)),
            pl.BlockSpec((1, N_KV_HEADS, 1, HEAD_DIM, tm), lambda i: (i // nt, 0, i % nt, 0, 0)),
        ),
        compiler_params=_params(("parallel",)),
        name="qkv_prep",
    )(proj, proj, row_tab, col_tab, q_gain, k_gain)


def _rope_tables(t):
    n_rows = t // GRID_W
    sec = HEAD_DIM // 2
    inv_freq = ROPE_THETA ** (-jnp.arange(0, sec, 2, dtype=F32) / sec)
    def planes(ang):
        c, s, z = jnp.cos(ang), jnp.sin(ang), jnp.zeros_like(ang)
        return jnp.stack([jnp.concatenate([c, c], 1), jnp.concatenate([-s, z], 1),
                          jnp.concatenate([z, s], 1)])

    row = planes(jnp.arange(n_rows, dtype=F32)[:, None] * inv_freq[None, :])
    col = planes(jnp.arange(GRID_W, dtype=F32)[:, None] * inv_freq[None, :])
    row_tab = jnp.concatenate([row, jnp.zeros_like(row)], axis=2)
    col_tab = jnp.concatenate([jnp.zeros_like(col), col], axis=2)
    return row_tab, col_tab


def _attn_kernel(bounded_ref, q_ref, k_ref, vt_ref, o_ref, acc_ref, l_ref, m_ref, *, nk):
    tq = q_ref.shape[0]
    tk = vt_ref.shape[-1]
    cols = Q_GROUP * tq
    sub = l_ref.shape[0]

    qt = jnp.concatenate(
        [q_ref[:, g * HEAD_DIM:(g + 1) * HEAD_DIM].astype(F32).T.astype(BF16)
         for g in range(Q_GROUP)], axis=1)

    def kv_tile(j):
        return k_ref[pl.ds(pl.multiple_of(j * tk, tk), tk), :], vt_ref[0, 0, j]

    def sublane_partial_sum(p):
        return jnp.sum(p.reshape(tk // sub, sub, cols), axis=0)

    acc_ref[...] = jnp.zeros_like(acc_ref)
    l_ref[...] = jnp.zeros_like(l_ref)

    @pl.when(bounded_ref[0] != 0)
    def _():
        group = min(ATT_KV_GROUP, nk)

        def body(jg, carry):
            l_sum = pv_sum = None
            for u in range(group):
                k, vt = kv_tile(jg * group + u)
                p = jnp.exp2(jnp.dot(k, qt, preferred_element_type=F32))
                l_u = sublane_partial_sum(p)
                pv_u = jnp.dot(vt, p.astype(BF16), preferred_element_type=F32)
                l_sum = l_u if l_sum is None else l_sum + l_u
                pv_sum = pv_u if pv_sum is None else pv_sum + pv_u
            l_ref[...] += l_sum
            acc_ref[...] += pv_sum
            return carry

        lax.fori_loop(0, nk // group, body, 0)

    @pl.when(bounded_ref[0] == 0)
    def _():
        m_ref[...] = jnp.full_like(m_ref, -jnp.inf)

        def body(j, carry):
            k, vt = kv_tile(j)
            s = jnp.dot(k, qt, preferred_element_type=F32)
            m_old = m_ref[...]
            m_new = jnp.maximum(m_old, jnp.max(s, axis=0, keepdims=True))
            alpha = jnp.exp2(m_old - m_new)
            p = jnp.exp2(s - m_new)
            m_ref[...] = m_new
            l_ref[...] = alpha * l_ref[...] + sublane_partial_sum(p)
            acc_ref[...] = alpha * acc_ref[...] + jnp.dot(vt, p.astype(BF16),
                                                          preferred_element_type=F32)
            return carry

        lax.fori_loop(0, nk, body, 0)

    out_t = acc_ref[...] * (1.0 / jnp.sum(l_ref[...], axis=0, keepdims=True))
    for g in range(Q_GROUP):
        o_ref[:, g * HEAD_DIM:(g + 1) * HEAD_DIM] = out_t[:, g * tq:(g + 1) * tq].T.astype(BF16)


def _scores_bounded(q_gain, k_gain):
    bound = (1.02 * LOG2_E * HEAD_DIM ** 0.5) * jnp.max(jnp.abs(q_gain)) * jnp.max(jnp.abs(k_gain))
    return (bound <= ATT_BOUNDED_MAX_LOG2).astype(jnp.int32).reshape(1)


def _attention(q, k, vt, bounded, b, t):
    n = b * t
    tq = ATT_TQ
    nq = t // tq
    nk = t // ATT_TK
    gw = Q_GROUP * HEAD_DIM
    return pl.pallas_call(
        functools.partial(_attn_kernel, nk=nk),
        out_shape=jax.ShapeDtypeStruct((n, ATTN_WIDTH), BF16),
        grid=(b, N_KV_HEADS, nq),
        in_specs=[
            pl.BlockSpec(memory_space=pltpu.SMEM),
            pl.BlockSpec((tq, gw), lambda bi, h, i: (bi * nq + i, h)),
            pl.BlockSpec((t, HEAD_DIM), lambda bi, h, i: (bi, h)),
            pl.BlockSpec((1, 1, nk, HEAD_DIM, ATT_TK), lambda bi, h, i: (bi, h, 0, 0, 0)),
        ],
        out_specs=pl.BlockSpec((tq, gw), lambda bi, h, i: (bi * nq + i, h)),
        scratch_shapes=[
            pltpu.VMEM((HEAD_DIM, Q_GROUP * tq), F32),
            pltpu.VMEM((SUBLANES, Q_GROUP * tq), F32),
            pltpu.VMEM((1, Q_GROUP * tq), F32),
        ],
        compiler_params=_params(("parallel", "parallel", "arbitrary")),
        name="attention",
    )(bounded, q, k, vt)


def _gla_dir(q_ref, k_ref, v_ref, lr_ref, w_ref, b_ref, tri_ref, mask_ref, o_ref, s_ref, cs_ref,
             reverse):
    c_sz = GLA_CHUNK
    n_chunks = GLA_TB // c_sz
    z = jnp.dot(lr_ref[...], w_ref[...], preferred_element_type=F32) + b_ref[...]
    yield
    softplus2 = jnp.log2(1.0 + jnp.exp2(jnp.abs(z) * (-LOG2_E)))
    la = (jnp.minimum(z, 0.0) * (LOG2_E / GLA_GATE_NORMALIZER)
          - softplus2 * (1.0 / GLA_GATE_NORMALIZER))
    hi = la.astype(BF16)
    lo = (la - hi.astype(F32)).astype(BF16)
    tri = tri_ref[...]
    cs = (jnp.dot(tri, hi, preferred_element_type=F32)
          + jnp.dot(tri, lo, preferred_element_type=F32))
    yield
    for h in range(GLA_HEADS):
        cs_ref[h] = cs[:, h * GLA_DK:(h + 1) * GLA_DK]
    half = c_sz // 2
    edge_start = 0 if reverse else half - 1
    edge_row = [2 * c if reverse else 2 * c + 1 for c in range(n_chunks)]

    qe = (q_ref[...].astype(F32) * (GLA_DK ** -0.5)) * jnp.exp2(cs)
    ke = k_ref[...].astype(F32) * jnp.exp2(-cs)
    qe_b = qe.astype(BF16)
    ke_b = ke.astype(BF16)
    v_b = v_ref[...]
    mask = mask_ref[...] > 0.5
    chunk_of_lane = lax.broadcasted_iota(jnp.int32, (GLA_DK, GLA_TB), 1) // c_sz
    order = range(n_chunks - 1, -1, -1) if reverse else range(n_chunks)

    heads = range(GLA_HEADS)
    ks = [slice(h * GLA_DK, (h + 1) * GLA_DK) for h in heads]
    vs = [slice(h * GLA_DV, (h + 1) * GLA_DV) for h in heads]
    yield

    a = [jnp.where(mask, lax.dot_general(qe_b[:, ks[h]], ke_b[:, ks[h]], (((1,), (1,)), ((), ())),
                                         preferred_element_type=F32), 0.0).astype(BF16)
         for h in heads]
    yield

    decay_t, prod = [], []
    for h in heads:
        edge = jnp.exp2(cs_ref[h, pl.ds(edge_start, 2 * n_chunks, stride=half), :])
        decay_t.append(jnp.tile(edge, (GLA_DK // (2 * n_chunks), 1)).T)
        kd = ke[:, ks[h]] * jnp.concatenate(
            [jnp.broadcast_to(edge[r:r + 1], (c_sz, GLA_DK)) for r in edge_row], axis=0)
        kd_t = kd.T
        lhs = jnp.concatenate(
            [a[h]]
            + [jnp.where(chunk_of_lane == c, kd_t, 0.0).astype(BF16) for c in range(n_chunks)],
            axis=0)
        prod.append(jnp.dot(lhs, v_b[:, vs[h]], preferred_element_type=F32))
        yield

    for h in heads:
        s = s_ref[h]
        for c in order:
            rs = slice(c * c_sz, (c + 1) * c_sz)
            r = edge_row[c]
            o_inter = jnp.dot(qe_b[rs, ks[h]], s.astype(BF16), preferred_element_type=F32)
            o_ref[rs, vs[h]] = prod[h][rs] + o_inter
            inc = prod[h][GLA_TB + c * GLA_DK:GLA_TB + (c + 1) * GLA_DK]
            s = decay_t[h][:, r:r + 1] * s + inc
        s_ref[h] = s
        yield


def _gla_kernel(qf_ref, kf_ref, vf_ref, lrf_ref, qb_ref, kb_ref, vb_ref, lrb_ref,
                wf_ref, bf_ref, wb_ref, bb_ref, trif_ref, trib_ref, maskf_ref, maskb_ref,
                of_ref, ob_ref, sf_ref, sb_ref, csf_ref, csb_ref):
    @pl.when(pl.program_id(1) == 0)
    def _():
        sf_ref[...] = jnp.zeros_like(sf_ref)
        sb_ref[...] = jnp.zeros_like(sb_ref)

    fwd = _gla_dir(qf_ref, kf_ref, vf_ref, lrf_ref, wf_ref, bf_ref, trif_ref, maskf_ref, of_ref,
                   sf_ref, csf_ref, reverse=False)
    bwd = _gla_dir(qb_ref, kb_ref, vb_ref, lrb_ref, wb_ref, bb_ref, trib_ref, maskb_ref, ob_ref,
                   sb_ref, csb_ref, reverse=True)
    for _ in itertools.zip_longest(fwd, bwd):
        pass


def _gla_consts():
    i = jnp.arange(GLA_TB)[:, None]
    j = jnp.arange(GLA_TB)[None, :]
    same = (i // GLA_CHUNK) == (j // GLA_CHUNK)
    tri_f = (same & (j <= i)).astype(F32)
    tri_b = (same & (j >= i)).astype(F32)
    mask_b = (same & (j > i)).astype(F32)
    return tri_f, tri_b, mask_b


def _gla(proj, w_f, b_f, w_b, b_b, b, t):
    n = b * t
    tb = GLA_TB
    nt = t // tb
    tri_f, tri_b, mask_b = _gla_consts()

    def fwd(width, col):
        return pl.BlockSpec((tb, width), lambda bi, i: (bi * nt + i, col // width))

    def bwd(width, col):
        return pl.BlockSpec((tb, width), lambda bi, i: (bi * nt + nt - 1 - i, col // width))

    def const(shape):
        return pl.BlockSpec(shape, lambda bi, i: (0, 0))

    pieces = ((GLA_K_WIDTH, COL_QG), (GLA_K_WIDTH, COL_KG), (GLA_V_WIDTH, COL_VG), (LANES, COL_LR))
    out_sd = jax.ShapeDtypeStruct((n, GLA_V_WIDTH), F32)
    return pl.pallas_call(
        _gla_kernel,
        out_shape=(out_sd, out_sd),
        grid=(b, nt),
        in_specs=[fwd(*p) for p in pieces] + [bwd(*p) for p in pieces] + [
            const((LANES, GLA_K_WIDTH)), const((1, GLA_K_WIDTH)),
            const((LANES, GLA_K_WIDTH)), const((1, GLA_K_WIDTH)),
            const((tb, tb)), const((tb, tb)), const((tb, tb)), const((tb, tb)),
        ],
        out_specs=(
            pl.BlockSpec((tb, GLA_V_WIDTH), lambda bi, i: (bi * nt + i, 0)),
            pl.BlockSpec((tb, GLA_V_WIDTH), lambda bi, i: (bi * nt + nt - 1 - i, 0)),
        ),
        scratch_shapes=[
            pltpu.VMEM((GLA_HEADS, GLA_DK, GLA_DV), F32),
            pltpu.VMEM((GLA_HEADS, GLA_DK, GLA_DV), F32),
            pltpu.VMEM((GLA_HEADS, tb, GLA_DK), F32),
            pltpu.VMEM((GLA_HEADS, tb, GLA_DK), F32),
        ],
        compiler_params=_params(("parallel", "arbitrary")),
        name="gla",
    )(*([proj] * 8), w_f, b_f, w_b, b_b, tri_f.astype(BF16), tri_b.astype(BF16), tri_f, mask_b)


def _merge_kernel(att_ref, of_ref, ob_ref, g_ref, gla_ref, glb_ref, wa_ref, wg_ref, gn_ref,
                  bma_ref, bmb_ref, o_ref):
    blocks = [slice(c * MERGE_TN, (c + 1) * MERGE_TN) for c in range(D_MODEL // MERGE_TN)]
    att = att_ref[...]
    ga_aa = [_sigmoid(gla_ref[:, cs].astype(F32) + bma_ref[:, cs])
             * jnp.dot(att, wa_ref[:, cs], preferred_element_type=F32) for cs in blocks]
    o = of_ref[...] + ob_ref[...]
    gn = gn_ref[...]
    on = jnp.concatenate(
        [_rms(o[:, h * GLA_DV:(h + 1) * GLA_DV], gn) for h in range(GLA_HEADS)], axis=1)
    g = g_ref[...].astype(F32)
    gl = (on * (g * _sigmoid(g))).astype(BF16)
    for cs, a_part in zip(blocks, ga_aa):
        bb = jnp.dot(gl, wg_ref[:, cs], preferred_element_type=F32)
        gb = _sigmoid(glb_ref[:, cs].astype(F32) + bmb_ref[:, cs])
        o_ref[:, cs] = (a_part + gb * bb).astype(BF16)


def _merge(att, o_f, o_b, proj, w_attn, w_gla, gla_norm, b_merge):
    n = att.shape[0]
    tm = 512
    row = lambda w, c=0: pl.BlockSpec((tm, w), lambda i: (i, c))
    const = lambda r, w, c=0: pl.BlockSpec((r, w), lambda i: (0, c))
    resident = lambda r, w: pl.BlockSpec((r, w), lambda i: (0, 0), pipeline_mode=pl.Buffered(1))
    return pl.pallas_call(
        _merge_kernel,
        out_shape=jax.ShapeDtypeStruct((n, D_MODEL), BF16),
        grid=(n // tm,),
        in_specs=[
            row(ATTN_WIDTH), row(GLA_V_WIDTH), row(GLA_V_WIDTH),
            row(GLA_V_WIDTH, COL_GG // GLA_V_WIDTH),
            row(D_MODEL, 0), row(D_MODEL, 1),
            resident(ATTN_WIDTH, D_MODEL), resident(GLA_V_WIDTH, D_MODEL), const(1, GLA_DV),
            const(1, D_MODEL, 0), const(1, D_MODEL, 1),
        ],
        out_specs=row(D_MODEL),
        compiler_params=_params(("parallel",)),
        name="merge",
    )(att, o_f, o_b, proj, proj, proj, w_attn, w_gla, gla_norm, b_merge, b_merge)


def _outproj_kernel(x_ref, m_ref, w_ref, g_ref, h_ref, hn_ref):
    h = x_ref[...] + jnp.dot(m_ref[...], w_ref[...], preferred_element_type=F32)
    h_ref[...] = h
    hn_ref[...] = _rms(h, g_ref[...]).astype(BF16)


def _outproj(x, mixed, w_out, gain_mlp):
    n = x.shape[0]
    tm = 512
    row = pl.BlockSpec((tm, D_MODEL), lambda i: (i, 0))
    return pl.pallas_call(
        _outproj_kernel,
        out_shape=(jax.ShapeDtypeStruct((n, D_MODEL), F32),
                   jax.ShapeDtypeStruct((n, D_MODEL), BF16)),
        grid=(n // tm,),
        in_specs=[row, row, pl.BlockSpec((D_MODEL, D_MODEL), lambda i: (0, 0)),
                  pl.BlockSpec((1, D_MODEL), lambda i: (0, 0))],
        out_specs=(row, row),
        compiler_params=_params(("parallel",)),
        name="out_proj",
    )(x, mixed, w_out, gain_mlp)


def _mlp_kernel(h_ref, hn_ref, wu_ref, wd_ref, gf_ref, o_ref, *, final_norm):
    j = pl.program_id(1)
    last = pl.num_programs(1) - 1

    def down():
        u = jnp.maximum(jnp.dot(hn_ref[...], wu_ref[...], preferred_element_type=F32), 0.0)
        return jnp.dot((u * u).astype(BF16), wd_ref[...], preferred_element_type=F32)

    @pl.when(j == 0)
    def _():
        o_ref[...] = h_ref[...] + down()

    @pl.when(jnp.logical_and(j > 0, j < last))
    def _():
        o_ref[...] += down()

    @pl.when(j == last)
    def _():
        o = o_ref[...] + down()
        o_ref[...] = _rms(o, gf_ref[...]) if final_norm else o


def _mlp(h, hn, w_up, w_down, gain_final, final_norm):
    n = h.shape[0]
    tm = 512
    tf = 2048
    assert D_FF // tf >= 2
    row = pl.BlockSpec((tm, D_MODEL), lambda i, j: (i, 0))
    vec = pl.BlockSpec((1, D_MODEL), lambda i, j: (0, 0))
    return pl.pallas_call(
        functools.partial(_mlp_kernel, final_norm=final_norm),
        out_shape=jax.ShapeDtypeStruct((n, D_MODEL), F32),
        grid=(n // tm, D_FF // tf),
        in_specs=[
            row, row,
            pl.BlockSpec((D_MODEL, tf), lambda i, j: (0, j)),
            pl.BlockSpec((tf, D_MODEL), lambda i, j: (j, 0)),
            vec,
        ],
        out_specs=row,
        compiler_params=_params(("parallel", "arbitrary"), MLP_VMEM_LIMIT),
        name="mlp",
    )(h, hn, w_up, w_down, gain_final)


_SRC_WIDTHS = (ATTN_WIDTH, KV_WIDTH, KV_WIDTH, GLA_K_WIDTH, GLA_K_WIDTH, GLA_V_WIDTH, GLA_V_WIDTH,
               GLA_GATE_RANK, GLA_GATE_RANK, N_BRANCHES * D_MODEL)
_SRC_OFF = tuple(sum(_SRC_WIDTHS[:i]) for i in range(len(_SRC_WIDTHS) + 1))
D_IN_PROJ = _SRC_OFF[-1]
_PROJ_MOVES = ((COL_QA, 0), (COL_KVA, 1), (COL_KVA + KV_WIDTH, 2), (COL_QG, 3), (COL_KG, 4),
               (COL_VG, 5), (COL_GG, 6))


def _wproj_kernel(wt_ref, o_ref):
    def put(dst, src, width):
        o_ref[:, dst:dst + width] = wt_ref[src:src + width, :].T.astype(BF16)

    put(COL_GATE, _SRC_OFF[9], N_BRANCHES * D_MODEL)
    for dst, piece in _PROJ_MOVES:
        put(dst, _SRC_OFF[piece], _SRC_WIDTHS[piece])
    lr = wt_ref[_SRC_OFF[7]:_SRC_OFF[7] + LANES, :].T
    lane = lax.broadcasted_iota(jnp.int32, lr.shape, 1)
    o_ref[:, COL_LR:COL_LR + LANES] = jnp.where(lane < 2 * GLA_GATE_RANK, lr, 0.0).astype(BF16)
    o_ref[:, COL_LR + LANES:] = jnp.zeros((o_ref.shape[0], PROJ_W - COL_LR - LANES), BF16)


def _wproj(w_in):
    wt = jnp.swapaxes(w_in, 0, 1)
    return pl.pallas_call(
        _wproj_kernel,
        out_shape=jax.ShapeDtypeStruct((D_MODEL, PROJ_W), BF16),
        grid=(D_MODEL // LANES,),
        in_specs=[pl.BlockSpec((D_IN_PROJ, LANES), lambda i: (0, i))],
        out_specs=pl.BlockSpec((LANES, PROJ_W), lambda i: (i, 0)),
        compiler_params=_params(("parallel",)),
        name="w_proj_layout",
    )(wt)


def _prep_layer(w_in, w_gate_up_fwd, b_gate_fwd, w_gate_up_bwd, b_gate_bwd, w_attn_proj,
                w_gla_proj, w_out, w_up, w_down):
    w_proj = _wproj(w_in)
    zf = jnp.zeros((LANES - GLA_GATE_RANK, GLA_K_WIDTH), F32)
    zb = jnp.zeros((LANES - 2 * GLA_GATE_RANK, GLA_K_WIDTH), F32)
    w_f = jnp.concatenate([w_gate_up_fwd, zf], axis=0).astype(BF16)
    w_b = jnp.concatenate([jnp.zeros((GLA_GATE_RANK, GLA_K_WIDTH), F32), w_gate_up_bwd, zb],
                          axis=0).astype(BF16)
    return dict(
        w_proj=w_proj, w_f=w_f, w_b=w_b,
        b_f=b_gate_fwd.reshape(1, -1), b_b=b_gate_bwd.reshape(1, -1),
        w_attn=w_attn_proj.astype(BF16), w_gla=w_gla_proj.astype(BF16),
        w_out=w_out.astype(BF16), w_up=w_up.astype(BF16), w_down=w_down.astype(BF16))


def _layer(x, b, t, lw, norm_mix, q_norm, k_norm, gla_norm, b_merge, norm_mlp, norm_final,
           final_norm):
    proj = _inproj(x, norm_mix.reshape(1, -1), lw["w_proj"])
    q, k, vt = _prep(proj, _rope_tables(t), q_norm.reshape(1, -1), k_norm.reshape(1, -1), b, t)
    att = _attention(q, k, vt, _scores_bounded(q_norm, k_norm), b, t)
    o_f, o_b = _gla(proj, lw["w_f"], lw["b_f"], lw["w_b"], lw["b_b"], b, t)
    mixed = _merge(att, o_f, o_b, proj, lw["w_attn"], lw["w_gla"], gla_norm.reshape(1, -1),
                   b_merge.reshape(1, -1))
    h, hn = _outproj(x, mixed, lw["w_out"], norm_mlp.reshape(1, -1))
    return _mlp(h, hn, lw["w_up"], lw["w_down"], norm_final.reshape(1, -1), final_norm)


def kernel(x_prompt, x_sample, norm_mix, w_in, q_norm, k_norm, w_gate_up_fwd, b_gate_fwd,
           w_gate_up_bwd, b_gate_bwd, gla_norm, w_attn_proj, w_gla_proj, b_merge, w_out,
           norm_mlp, w_up, w_down, norm_final):
    layers = [
        _prep_layer(w_in[l], w_gate_up_fwd[l], b_gate_fwd[l], w_gate_up_bwd[l], b_gate_bwd[l],
                    w_attn_proj[l], w_gla_proj[l], w_out[l], w_up[l], w_down[l])
        for l in range(DEPTH)]

    def trunk(x):
        b, t, d = x.shape
        y = x.reshape(b * t, d)
        for l in range(DEPTH):
            y = _layer(y, b, t, layers[l], norm_mix[l], q_norm[l], k_norm[l], gla_norm[l],
                       b_merge[l], norm_mlp[l], norm_final, final_norm=(l == DEPTH - 1))
        return y.reshape(b, t, d)

    return trunk(x_prompt), trunk(x_sample)
```

```python
import functools
import itertools

import jax
import jax.numpy as jnp
from jax import lax
from jax.experimental import pallas as pl
from jax.experimental.pallas import tpu as pltpu

F32 = jnp.float32
BF16 = jnp.bfloat16

D_MODEL = 2048
DEPTH = 1
GRID_W = 64
HEAD_DIM = 128
N_Q_HEADS = 8
N_KV_HEADS = 2
Q_GROUP = N_Q_HEADS // N_KV_HEADS
ATTN_WIDTH = N_Q_HEADS * HEAD_DIM
KV_WIDTH = N_KV_HEADS * HEAD_DIM
ROPE_THETA = 10000.0
GLA_HEADS = 4
GLA_DK = 128
GLA_DV = 256
GLA_K_WIDTH = GLA_HEADS * GLA_DK
GLA_V_WIDTH = GLA_HEADS * GLA_DV
GLA_GATE_RANK = 16
GLA_GATE_NORMALIZER = 16.0
GLA_CHUNK = 64
N_BRANCHES = 2
D_FF = 4 * D_MODEL
NORM_EPS = 1e-6

LANES = 128
SUBLANES = 8
VMEM_LIMIT = 48 * 1024 * 1024
BIG_VMEM_LIMIT = 58 * 1024 * 1024

COL_GATE = 0
COL_QA = COL_GATE + N_BRANCHES * D_MODEL
COL_VG = COL_QA + ATTN_WIDTH
COL_GG = COL_VG + GLA_V_WIDTH
COL_KVA = COL_GG + GLA_V_WIDTH
COL_QG = COL_KVA + 2 * KV_WIDTH
COL_KG = COL_QG + GLA_K_WIDTH
COL_LR = COL_KG + GLA_K_WIDTH
PROJ_TN = 1792
PROJ_W = 5 * PROJ_TN

GLA_TB = 256
MERGE_TN = D_MODEL // 2
ATT_TQ = 256
ATT_TK = 1024
ATT_KV_GROUP = 4
LOG2_E = 1.4426950408889634
ATT_BOUNDED_MAX_LOG2 = 60.0


def _params(sem, vmem_limit=VMEM_LIMIT):
    return pltpu.CompilerParams(dimension_semantics=sem, vmem_limit_bytes=vmem_limit)


def _sigmoid(x):
    return 1.0 / (1.0 + jnp.exp(-x))


def _rms(x, gain):
    ms = jnp.mean(x * x, axis=-1, keepdims=True)
    return x * lax.rsqrt(ms + NORM_EPS) * gain


def _inproj_kernel(x_ref, g_ref, w_ref, o_ref, xn_ref):
    @pl.when(pl.program_id(1) == 0)
    def _():
        xn_ref[...] = _rms(x_ref[...], g_ref[...]).astype(BF16)

    o_ref[...] = jnp.dot(xn_ref[...], w_ref[...], preferred_element_type=F32).astype(o_ref.dtype)


def _inproj(x, gain, w):
    n = x.shape[0]
    tm = 1024
    return pl.pallas_call(
        _inproj_kernel,
        out_shape=jax.ShapeDtypeStruct((n, PROJ_W), BF16),
        grid=(n // tm, PROJ_W // PROJ_TN),
        in_specs=[
            pl.BlockSpec((tm, D_MODEL), lambda i, j: (i, 0)),
            pl.BlockSpec((1, D_MODEL), lambda i, j: (0, 0)),
            pl.BlockSpec((D_MODEL, PROJ_TN), lambda i, j: (0, j)),
        ],
        out_specs=pl.BlockSpec((tm, PROJ_TN), lambda i, j: (i, j)),
        scratch_shapes=[pltpu.VMEM((tm, D_MODEL), BF16)],
        compiler_params=_params(("parallel", "arbitrary"), BIG_VMEM_LIMIT),
        name="in_proj",
    )(x, gain, w)


def _prep_kernel(q_ref, kv_ref, rt_ref, ct_ref, qg_ref, kg_ref, qo_ref, ko_ref, vt_ref):
    rows_per_tile = rt_ref.shape[1]

    def table(c):
        rt = rt_ref[c]
        by_row = jnp.concatenate(
            [jnp.broadcast_to(rt[g:g + 1], (GRID_W, HEAD_DIM)) for g in range(rows_per_tile)],
            axis=0)
        return by_row + jnp.tile(ct_ref[c], (rows_per_tile, 1))

    cos, sa, sb = table(0), table(1), table(2)

    def norm_rope(x, gain):
        y = _rms(x.astype(F32), gain)
        return y * cos + pltpu.roll(y, 96, 1) * sa + pltpu.roll(y, 32, 1) * sb

    qg = qg_ref[...]
    kg = kg_ref[...]
    for h in range(N_Q_HEADS):
        sl = slice(h * HEAD_DIM, (h + 1) * HEAD_DIM)
        qo_ref[:, sl] = (norm_rope(q_ref[:, sl], qg) * (LOG2_E * HEAD_DIM ** -0.5)).astype(BF16)
    for h in range(N_KV_HEADS):
        sl = slice(h * HEAD_DIM, (h + 1) * HEAD_DIM)
        ko_ref[:, sl] = norm_rope(kv_ref[:, sl], kg).astype(BF16)
        v = kv_ref[:, KV_WIDTH + h * HEAD_DIM:KV_WIDTH + (h + 1) * HEAD_DIM]
        vt_ref[0, h, 0] = v.astype(F32).T.astype(BF16)


def _prep(proj, tables, q_gain, k_gain, b, t):
    n = b * t
    tm = ATT_TK
    nt = t // tm
    row_tab, col_tab = tables
    rows_per_tile = tm // GRID_W
    row_spec = pl.BlockSpec((3, rows_per_tile, HEAD_DIM), lambda i: (0, i % nt, 0))
    col_spec = pl.BlockSpec((3, GRID_W, HEAD_DIM), lambda i: (0, 0, 0))
    gain_spec = pl.BlockSpec((1, HEAD_DIM), lambda i: (0, 0))
    return pl.pallas_call(
        _prep_kernel,
        out_shape=(
            jax.ShapeDtypeStruct((n, ATTN_WIDTH), BF16),
            jax.ShapeDtypeStruct((n, KV_WIDTH), BF16),
            jax.ShapeDtypeStruct((b, N_KV_HEADS, nt, HEAD_DIM, tm), BF16),
        ),
        grid=(n // tm,),
        in_specs=[
            pl.BlockSpec((tm, ATTN_WIDTH), lambda i: (i, COL_QA // ATTN_WIDTH)),
            pl.BlockSpec((tm, 2 * KV_WIDTH), lambda i: (i, COL_KVA // (2 * KV_WIDTH))),
            row_spec, col_spec, gain_spec, gain_spec,
        ],
        out_specs=(
            pl.BlockSpec((tm, ATTN_WIDTH), lambda i: (i, 0)),
            pl.BlockSpec((tm, KV_WIDTH), lambda i: (i, 0)),
            pl.BlockSpec((1, N_KV_HEADS, 1, HEAD_DIM, tm), lambda i: (i // nt, 0, i % nt, 0, 0)),
        ),
        compiler_params=_params(("parallel",)),
        name="qkv_prep",
    )(proj, proj, row_tab, col_tab, q_gain, k_gain)


def _rope_tables(t):
    n_rows = t // GRID_W
    sec = HEAD_DIM // 2
    inv_freq = ROPE_THETA ** (-jnp.arange(0, sec, 2, dtype=F32) / sec)
    def planes(ang):
        c, s, z = jnp.cos(ang), jnp.sin(ang), jnp.zeros_like(ang)
        return jnp.stack([jnp.concatenate([c, c], 1), jnp.concatenate([-s, z], 1),
                          jnp.concatenate([z, s], 1)])

    row = planes(jnp.arange(n_rows, dtype=F32)[:, None] * inv_freq[None, :])
    col = planes(jnp.arange(GRID_W, dtype=F32)[:, None] * inv_freq[None, :])
    row_tab = jnp.concatenate([row, jnp.zeros_like(row)], axis=2)
    col_tab = jnp.concatenate([jnp.zeros_like(col), col], axis=2)
    return row_tab, col_tab


def _attn_kernel(bounded_ref, q_ref, k_ref, vt_ref, o_ref, acc_ref, l_ref, m_ref, *, nk):
    tq = q_ref.shape[0]
    tk = vt_ref.shape[-1]
    cols = Q_GROUP * tq
    sub = l_ref.shape[0]

    qt = jnp.concatenate(
        [q_ref[:, g * HEAD_DIM:(g + 1) * HEAD_DIM].astype(F32).T.astype(BF16)
         for g in range(Q_GROUP)], axis=1)

    def kv_tile(j):
        return k_ref[pl.ds(pl.multiple_of(j * tk, tk), tk), :], vt_ref[0, 0, j]

    def sublane_partial_sum(p):
        return jnp.sum(p.reshape(tk // sub, sub, cols), axis=0)

    acc_ref[...] = jnp.zeros_like(acc_ref)
    l_ref[...] = jnp.zeros_like(l_ref)

    @pl.when(bounded_ref[0] != 0)
    def _():
        group = min(ATT_KV_GROUP, nk)

        def body(jg, carry):
            l_sum = pv_sum = None
            for u in range(group):
                k, vt = kv_tile(jg * group + u)
                p = jnp.exp2(jnp.dot(k, qt, preferred_element_type=F32))
                l_u = sublane_partial_sum(p)
                pv_u = jnp.dot(vt, p.astype(BF16), preferred_element_type=F32)
                l_sum = l_u if l_sum is None else l_sum + l_u
                pv_sum = pv_u if pv_sum is None else pv_sum + pv_u
            l_ref[...] += l_sum
            acc_ref[...] += pv_sum
            return carry

        lax.fori_loop(0, nk // group, body, 0)

    @pl.when(bounded_ref[0] == 0)
    def _():
        m_ref[...] = jnp.full_like(m_ref, -jnp.inf)

        def body(j, carry):
            k, vt = kv_tile(j)
            s = jnp.dot(k, qt, preferred_element_type=F32)
            m_old = m_ref[...]
            m_new = jnp.maximum(m_old, jnp.max(s, axis=0, keepdims=True))
            alpha = jnp.exp2(m_old - m_new)
            p = jnp.exp2(s - m_new)
            m_ref[...] = m_new
            l_ref[...] = alpha * l_ref[...] + sublane_partial_sum(p)
            acc_ref[...] = alpha * acc_ref[...] + jnp.dot(vt, p.astype(BF16),
                                                          preferred_element_type=F32)
            return carry

        lax.fori_loop(0, nk, body, 0)

    out_t = acc_ref[...] * (1.0 / jnp.sum(l_ref[...], axis=0, keepdims=True))
    for g in range(Q_GROUP):
        o_ref[:, g * HEAD_DIM:(g + 1) * HEAD_DIM] = out_t[:, g * tq:(g + 1) * tq].T.astype(BF16)


def _scores_bounded(q_gain, k_gain):
    bound = (1.02 * LOG2_E * HEAD_DIM ** 0.5) * jnp.max(jnp.abs(q_gain)) * jnp.max(jnp.abs(k_gain))
    return (bound <= ATT_BOUNDED_MAX_LOG2).astype(jnp.int32).reshape(1)


def _attention(q, k, vt, bounded, b, t):
    n = b * t
    tq = ATT_TQ
    nq = t // tq
    nk = t // ATT_TK
    gw = Q_GROUP * HEAD_DIM
    return pl.pallas_call(
        functools.partial(_attn_kernel, nk=nk),
        out_shape=jax.ShapeDtypeStruct((n, ATTN_WIDTH), BF16),
        grid=(b, N_KV_HEADS, nq),
        in_specs=[
            pl.BlockSpec(memory_space=pltpu.SMEM),
            pl.BlockSpec((tq, gw), lambda bi, h, i: (bi * nq + i, h)),
            pl.BlockSpec((t, HEAD_DIM), lambda bi, h, i: (bi, h)),
            pl.BlockSpec((1, 1, nk, HEAD_DIM, ATT_TK), lambda bi, h, i: (bi, h, 0, 0, 0)),
        ],
        out_specs=pl.BlockSpec((tq, gw), lambda bi, h, i: (bi * nq + i, h)),
        scratch_shapes=[
            pltpu.VMEM((HEAD_DIM, Q_GROUP * tq), F32),
            pltpu.VMEM((SUBLANES, Q_GROUP * tq), F32),
            pltpu.VMEM((1, Q_GROUP * tq), F32),
        ],
        compiler_params=_params(("parallel", "parallel", "arbitrary")),
        name="attention",
    )(bounded, q, k, vt)


def _gla_dir(q_ref, k_ref, v_ref, lr_ref, w_ref, b_ref, tri_ref, mask_ref, o_ref, s_ref, cs_ref,
             reverse):
    c_sz = GLA_CHUNK
    n_chunks = GLA_TB // c_sz
    z = jnp.dot(lr_ref[...], w_ref[...], preferred_element_type=F32) + b_ref[...]
    yield
    softplus2 = jnp.log2(1.0 + jnp.exp2(jnp.abs(z) * (-LOG2_E)))
    la = (jnp.minimum(z, 0.0) * (LOG2_E / GLA_GATE_NORMALIZER)
          - softplus2 * (1.0 / GLA_GATE_NORMALIZER))
    hi = la.astype(BF16)
    lo = (la - hi.astype(F32)).astype(BF16)
    tri = tri_ref[...]
    cs = (jnp.dot(tri, hi, preferred_element_type=F32)
          + jnp.dot(tri, lo, preferred_element_type=F32))
    yield
    for h in range(GLA_HEADS):
        cs_ref[h] = cs[:, h * GLA_DK:(h + 1) * GLA_DK]
    half = c_sz // 2
    edge_start = 0 if reverse else half - 1
    edge_row = [2 * c if reverse else 2 * c + 1 for c in range(n_chunks)]

    qe = (q_ref[...].astype(F32) * (GLA_DK ** -0.5)) * jnp.exp2(cs)
    ke = k_ref[...].astype(F32) * jnp.exp2(-cs)
    qe_b = qe.astype(BF16)
    ke_b = ke.astype(BF16)
    v_b = v_ref[...]
    mask = mask_ref[...] > 0.5
    chunk_of_lane = lax.broadcasted_iota(jnp.int32, (GLA_DK, GLA_TB), 1) // c_sz
    order = range(n_chunks - 1, -1, -1) if reverse else range(n_chunks)

    heads = range(GLA_HEADS)
    ks = [slice(h * GLA_DK, (h + 1) * GLA_DK) for h in heads]
    vs = [slice(h * GLA_DV, (h + 1) * GLA_DV) for h in heads]
    yield

    a = [jnp.where(mask, lax.dot_general(qe_b[:, ks[h]], ke_b[:, ks[h]], (((1,), (1,)), ((), ())),
                                         preferred_element_type=F32), 0.0).astype(BF16)
         for h in heads]
    yield

    decay_t, prod = [], []

    def recurrence(h):
        s = s_ref[h]
        for c in order:
            rs = slice(c * c_sz, (c + 1) * c_sz)
            r = edge_row[c]
            o_inter = jnp.dot(qe_b[rs, ks[h]], s.astype(BF16), preferred_element_type=F32)
            o_ref[rs, vs[h]] = (prod[h][rs] + o_inter).astype(o_ref.dtype)
            inc = prod[h][GLA_TB + c * GLA_DK:GLA_TB + (c + 1) * GLA_DK]
            s = decay_t[h][:, r:r + 1] * s + inc
        s_ref[h] = s

    for h in heads:
        edge = jnp.exp2(cs_ref[h, pl.ds(edge_start, 2 * n_chunks, stride=half), :])
        decay_t.append(jnp.tile(edge, (GLA_DK // (2 * n_chunks), 1)).T)
        kd = ke[:, ks[h]] * jnp.concatenate(
            [jnp.broadcast_to(edge[r:r + 1], (c_sz, GLA_DK)) for r in edge_row], axis=0)
        kd_t = kd.T
        lhs = jnp.concatenate(
            [a[h]]
            + [jnp.where(chunk_of_lane == c, kd_t, 0.0).astype(BF16) for c in range(n_chunks)],
            axis=0)
        prod.append(jnp.dot(lhs, v_b[:, vs[h]], preferred_element_type=F32))
        yield

    for h in heads:
        recurrence(h)
        yield


def _gla_kernel(qf_ref, kf_ref, vf_ref, lrf_ref, qb_ref, kb_ref, vb_ref, lrb_ref,
                wf_ref, bf_ref, wb_ref, bb_ref, trif_ref, trib_ref, maskf_ref, maskb_ref,
                of_ref, ob_ref, sf_ref, sb_ref, csf_ref, csb_ref):
    @pl.when(pl.program_id(1) == 0)
    def _():
        sf_ref[...] = jnp.zeros_like(sf_ref)
        sb_ref[...] = jnp.zeros_like(sb_ref)

    fwd = _gla_dir(qf_ref, kf_ref, vf_ref, lrf_ref, wf_ref, bf_ref, trif_ref, maskf_ref, of_ref,
                   sf_ref, csf_ref, reverse=False)
    bwd = _gla_dir(qb_ref, kb_ref, vb_ref, lrb_ref, wb_ref, bb_ref, trib_ref, maskb_ref, ob_ref,
                   sb_ref, csb_ref, reverse=True)
    for _ in itertools.zip_longest(fwd, bwd):
        pass


def _gla_consts():
    i = jnp.arange(GLA_TB)[:, None]
    j = jnp.arange(GLA_TB)[None, :]
    same = (i // GLA_CHUNK) == (j // GLA_CHUNK)
    tri_f = (same & (j <= i)).astype(F32)
    tri_b = (same & (j >= i)).astype(F32)
    mask_b = (same & (j > i)).astype(F32)
    return tri_f, tri_b, mask_b


def _gla(proj, w_f, b_f, w_b, b_b, b, t):
    n = b * t
    tb = GLA_TB
    nt = t // tb
    tri_f, tri_b, mask_b = _gla_consts()

    def fwd(width, col):
        return pl.BlockSpec((tb, width), lambda bi, i: (bi * nt + i, col // width))

    def bwd(width, col):
        return pl.BlockSpec((tb, width), lambda bi, i: (bi * nt + nt - 1 - i, col // width))

    def const(shape):
        return pl.BlockSpec(shape, lambda bi, i: (0, 0))

    pieces = ((GLA_K_WIDTH, COL_QG), (GLA_K_WIDTH, COL_KG), (GLA_V_WIDTH, COL_VG), (LANES, COL_LR))
    out_sd = jax.ShapeDtypeStruct((n, GLA_V_WIDTH), BF16)
    return pl.pallas_call(
        _gla_kernel,
        out_shape=(out_sd, out_sd),
        grid=(b, nt),
        in_specs=[fwd(*p) for p in pieces] + [bwd(*p) for p in pieces] + [
            const((LANES, GLA_K_WIDTH)), const((1, GLA_K_WIDTH)),
            const((LANES, GLA_K_WIDTH)), const((1, GLA_K_WIDTH)),
            const((tb, tb)), const((tb, tb)), const((tb, tb)), const((tb, tb)),
        ],
        out_specs=(
            pl.BlockSpec((tb, GLA_V_WIDTH), lambda bi, i: (bi * nt + i, 0)),
            pl.BlockSpec((tb, GLA_V_WIDTH), lambda bi, i: (bi * nt + nt - 1 - i, 0)),
        ),
        scratch_shapes=[
            pltpu.VMEM((GLA_HEADS, GLA_DK, GLA_DV), F32),
            pltpu.VMEM((GLA_HEADS, GLA_DK, GLA_DV), F32),
            pltpu.VMEM((GLA_HEADS, tb, GLA_DK), F32),
            pltpu.VMEM((GLA_HEADS, tb, GLA_DK), F32),
        ],
        compiler_params=_params(("parallel", "arbitrary")),
        name="gla",
    )(*([proj] * 8), w_f, b_f, w_b, b_b, tri_f.astype(BF16), tri_b.astype(BF16), tri_f, mask_b)


def _merge_kernel(att_ref, of_ref, ob_ref, g_ref, gla_ref, glb_ref, wa_ref, wg_ref, gn_ref,
                  bma_ref, bmb_ref, o_ref):
    blocks = [slice(c * MERGE_TN, (c + 1) * MERGE_TN) for c in range(D_MODEL // MERGE_TN)]
    att = att_ref[...]
    ga_aa = [_sigmoid(gla_ref[:, cs].astype(F32) + bma_ref[:, cs])
             * jnp.dot(att, wa_ref[:, cs], preferred_element_type=F32) for cs in blocks]
    o = of_ref[...].astype(F32) + ob_ref[...].astype(F32)
    gn = gn_ref[...]
    on = jnp.concatenate(
        [_rms(o[:, h * GLA_DV:(h + 1) * GLA_DV], gn) for h in range(GLA_HEADS)], axis=1)
    g = g_ref[...].astype(F32)
    gl = (on * (g * _sigmoid(g))).astype(BF16)
    for cs, a_part in zip(blocks, ga_aa):
        bb = jnp.dot(gl, wg_ref[:, cs], preferred_element_type=F32)
        gb = _sigmoid(glb_ref[:, cs].astype(F32) + bmb_ref[:, cs])
        o_ref[:, cs] = (a_part + gb * bb).astype(BF16)


def _merge(att, o_f, o_b, proj, w_attn, w_gla, gla_norm, b_merge):
    n = att.shape[0]
    tm = 512
    row = lambda w, c=0: pl.BlockSpec((tm, w), lambda i: (i, c))
    const = lambda r, w, c=0: pl.BlockSpec((r, w), lambda i: (0, c))
    resident = lambda r, w: pl.BlockSpec((r, w), lambda i: (0, 0), pipeline_mode=pl.Buffered(1))
    return pl.pallas_call(
        _merge_kernel,
        out_shape=jax.ShapeDtypeStruct((n, D_MODEL), BF16),
        grid=(n // tm,),
        in_specs=[
            row(ATTN_WIDTH), row(GLA_V_WIDTH), row(GLA_V_WIDTH),
            row(GLA_V_WIDTH, COL_GG // GLA_V_WIDTH),
            row(D_MODEL, 0), row(D_MODEL, 1),
            resident(ATTN_WIDTH, D_MODEL), resident(GLA_V_WIDTH, D_MODEL), const(1, GLA_DV),
            const(1, D_MODEL, 0), const(1, D_MODEL, 1),
        ],
        out_specs=row(D_MODEL),
        compiler_params=_params(("parallel",)),
        name="merge",
    )(att, o_f, o_b, proj, proj, proj, w_attn, w_gla, gla_norm, b_merge, b_merge)


def _outproj_kernel(x_ref, m_ref, w_ref, g_ref, h_ref, hn_ref):
    h = x_ref[...] + jnp.dot(m_ref[...], w_ref[...], preferred_element_type=F32)
    h_ref[...] = h
    hn_ref[...] = _rms(h, g_ref[...]).astype(BF16)


def _outproj(x, mixed, w_out, gain_mlp):
    n = x.shape[0]
    tm = 512
    row = pl.BlockSpec((tm, D_MODEL), lambda i: (i, 0))
    return pl.pallas_call(
        _outproj_kernel,
        out_shape=(jax.ShapeDtypeStruct((n, D_MODEL), F32),
                   jax.ShapeDtypeStruct((n, D_MODEL), BF16)),
        grid=(n // tm,),
        in_specs=[row, row, pl.BlockSpec((D_MODEL, D_MODEL), lambda i: (0, 0)),
                  pl.BlockSpec((1, D_MODEL), lambda i: (0, 0))],
        out_specs=(row, row),
        compiler_params=_params(("parallel",)),
        name="out_proj",
    )(x, mixed, w_out, gain_mlp)


def _mlp_kernel(h_ref, hn_ref, wu_ref, wd_ref, gf_ref, o_ref, *, final_norm):
    j = pl.program_id(1)
    last = pl.num_programs(1) - 1

    def down():
        u = jnp.maximum(jnp.dot(hn_ref[...], wu_ref[...], preferred_element_type=F32), 0.0)
        return jnp.dot((u * u).astype(BF16), wd_ref[...], preferred_element_type=F32)

    @pl.when(j == 0)
    def _():
        o_ref[...] = h_ref[...] + down()

    @pl.when(jnp.logical_and(j > 0, j < last))
    def _():
        o_ref[...] += down()

    @pl.when(j == last)
    def _():
        o = o_ref[...] + down()
        o_ref[...] = _rms(o, gf_ref[...]) if final_norm else o


def _mlp(h, hn, w_up, w_down, gain_final, final_norm):
    n = h.shape[0]
    tm = 512
    tf = 2048
    assert D_FF // tf >= 2
    row = pl.BlockSpec((tm, D_MODEL), lambda i, j: (i, 0))
    vec = pl.BlockSpec((1, D_MODEL), lambda i, j: (0, 0))
    return pl.pallas_call(
        functools.partial(_mlp_kernel, final_norm=final_norm),
        out_shape=jax.ShapeDtypeStruct((n, D_MODEL), F32),
        grid=(n // tm, D_FF // tf),
        in_specs=[
            row, row,
            pl.BlockSpec((D_MODEL, tf), lambda i, j: (0, j)),
            pl.BlockSpec((tf, D_MODEL), lambda i, j: (j, 0)),
            vec,
        ],
        out_specs=row,
        compiler_params=_params(("parallel", "arbitrary"), BIG_VMEM_LIMIT),
        name="mlp",
    )(h, hn, w_up, w_down, gain_final)


_SRC_WIDTHS = (ATTN_WIDTH, KV_WIDTH, KV_WIDTH, GLA_K_WIDTH, GLA_K_WIDTH, GLA_V_WIDTH, GLA_V_WIDTH,
               GLA_GATE_RANK, GLA_GATE_RANK, N_BRANCHES * D_MODEL)
_SRC_OFF = tuple(sum(_SRC_WIDTHS[:i]) for i in range(len(_SRC_WIDTHS) + 1))
D_IN_PROJ = _SRC_OFF[-1]
_PROJ_MOVES = ((COL_QA, 0), (COL_KVA, 1), (COL_KVA + KV_WIDTH, 2), (COL_QG, 3), (COL_KG, 4),
               (COL_VG, 5), (COL_GG, 6))


def _wproj_kernel(wt_ref, o_ref):
    def put(dst, src, width):
        o_ref[:, dst:dst + width] = wt_ref[src:src + width, :].T.astype(BF16)

    put(COL_GATE, _SRC_OFF[9], N_BRANCHES * D_MODEL)
    for dst, piece in _PROJ_MOVES:
        put(dst, _SRC_OFF[piece], _SRC_WIDTHS[piece])
    lr = wt_ref[_SRC_OFF[7]:_SRC_OFF[7] + LANES, :].T
    lane = lax.broadcasted_iota(jnp.int32, lr.shape, 1)
    o_ref[:, COL_LR:COL_LR + LANES] = jnp.where(lane < 2 * GLA_GATE_RANK, lr, 0.0).astype(BF16)
    o_ref[:, COL_LR + LANES:] = jnp.zeros((o_ref.shape[0], PROJ_W - COL_LR - LANES), BF16)


def _wproj(w_in):
    wt = jnp.swapaxes(w_in, 0, 1)
    return pl.pallas_call(
        _wproj_kernel,
        out_shape=jax.ShapeDtypeStruct((D_MODEL, PROJ_W), BF16),
        grid=(D_MODEL // LANES,),
        in_specs=[pl.BlockSpec((D_IN_PROJ, LANES), lambda i: (0, i))],
        out_specs=pl.BlockSpec((LANES, PROJ_W), lambda i: (i, 0)),
        compiler_params=_params(("parallel",)),
        name="w_proj_layout",
    )(wt)


def _prep_layer(w_in, w_gate_up_fwd, b_gate_fwd, w_gate_up_bwd, b_gate_bwd, w_attn_proj,
                w_gla_proj, w_out, w_up, w_down):
    w_proj = _wproj(w_in)
    zf = jnp.zeros((LANES - GLA_GATE_RANK, GLA_K_WIDTH), F32)
    zb = jnp.zeros((LANES - 2 * GLA_GATE_RANK, GLA_K_WIDTH), F32)
    w_f = jnp.concatenate([w_gate_up_fwd, zf], axis=0).astype(BF16)
    w_b = jnp.concatenate([jnp.zeros((GLA_GATE_RANK, GLA_K_WIDTH), F32), w_gate_up_bwd, zb],
                          axis=0).astype(BF16)
    return dict(
        w_proj=w_proj, w_f=w_f, w_b=w_b,
        b_f=b_gate_fwd.reshape(1, -1), b_b=b_gate_bwd.reshape(1, -1),
        w_attn=w_attn_proj.astype(BF16), w_gla=w_gla_proj.astype(BF16),
        w_out=w_out.astype(BF16), w_up=w_up.astype(BF16), w_down=w_down.astype(BF16))


def _layer(x, b, t, lw, norm_mix, q_norm, k_norm, gla_norm, b_merge, norm_mlp, norm_final,
           final_norm):
    proj = _inproj(x, norm_mix.reshape(1, -1), lw["w_proj"])
    q, k, vt = _prep(proj, _rope_tables(t), q_norm.reshape(1, -1), k_norm.reshape(1, -1), b, t)
    att = _attention(q, k, vt, _scores_bounded(q_norm, k_norm), b, t)
    o_f, o_b = _gla(proj, lw["w_f"], lw["b_f"], lw["w_b"], lw["b_b"], b, t)
    mixed = _merge(att, o_f, o_b, proj, lw["w_attn"], lw["w_gla"], gla_norm.reshape(1, -1),
                   b_merge.reshape(1, -1))
    h, hn = _outproj(x, mixed, lw["w_out"], norm_mlp.reshape(1, -1))
    return _mlp(h, hn, lw["w_up"], lw["w_down"], norm_final.reshape(1, -1), final_norm)


def kernel(x_prompt, x_sample, norm_mix, w_in, q_norm, k_norm, w_gate_up_fwd, b_gate_fwd,
           w_gate_up_bwd, b_gate_bwd, gla_norm, w_attn_proj, w_gla_proj, b_merge, w_out,
           norm_mlp, w_up, w_down, norm_final):
    layers = [
        _prep_layer(w_in[l], w_gate_up_fwd[l], b_gate_fwd[l], w_gate_up_bwd[l], b_gate_bwd[l],
                    w_attn_proj[l], w_gla_proj[l], w_out[l], w_up[l], w_down[l])
        for l in range(DEPTH)]

    def trunk(x):
        b, t, d = x.shape
        y = x.reshape(b * t, d)
        for l in range(DEPTH):
            y = _layer(y, b, t, layers[l], norm_mix[l], q_norm[l], k_norm[l], gla_norm[l],
                       b_merge[l], norm_mlp[l], norm_final, final_norm=(l == DEPTH - 1))
        return y.reshape(b, t, d)

    return trunk(x_prompt), trunk(x_sample)
```

```python
import functools
import itertools

import jax
import jax.numpy as jnp
from jax import lax
from jax.experimental import pallas as pl
from jax.experimental.pallas import tpu as pltpu

F32 = jnp.float32
BF16 = jnp.bfloat16

D_MODEL = 2048
DEPTH = 1
GRID_W = 64
HEAD_DIM = 128
N_Q_HEADS = 8
N_KV_HEADS = 2
Q_GROUP = N_Q_HEADS // N_KV_HEADS
ATTN_WIDTH = N_Q_HEADS * HEAD_DIM
KV_WIDTH = N_KV_HEADS * HEAD_DIM
ROPE_THETA = 10000.0
GLA_HEADS = 4
GLA_DK = 128
GLA_DV = 256
GLA_K_WIDTH = GLA_HEADS * GLA_DK
GLA_V_WIDTH = GLA_HEADS * GLA_DV
GLA_GATE_RANK = 16
GLA_GATE_NORMALIZER = 16.0
GLA_CHUNK = 64
N_BRANCHES = 2
D_FF = 4 * D_MODEL
NORM_EPS = 1e-6

LANES = 128
SUBLANES = 8
VMEM_LIMIT = 48 * 1024 * 1024
BIG_VMEM_LIMIT = 58 * 1024 * 1024

COL_GATE = 0
COL_QA = COL_GATE + N_BRANCHES * D_MODEL
COL_VG = COL_QA + ATTN_WIDTH
COL_GG = COL_VG + GLA_V_WIDTH
COL_KVA = COL_GG + GLA_V_WIDTH
COL_QG = COL_KVA + 2 * KV_WIDTH
COL_KG = COL_QG + GLA_K_WIDTH
COL_LR = COL_KG + GLA_K_WIDTH
PROJ_TN = 1792
PROJ_W = 5 * PROJ_TN

GLA_TB = 256
MERGE_TN = D_MODEL // 2
ATT_TQ = 256
ATT_TK = 2048
ATT_KV_GROUP = 2
LOG2_E = 1.4426950408889634
ATT_BOUNDED_MAX_LOG2 = 60.0


def _params(sem, vmem_limit=VMEM_LIMIT):
    return pltpu.CompilerParams(dimension_semantics=sem, vmem_limit_bytes=vmem_limit)


def _sigmoid(x):
    return 1.0 / (1.0 + jnp.exp(-x))


def _rms(x, gain):
    ms = jnp.mean(x * x, axis=-1, keepdims=True)
    return x * lax.rsqrt(ms + NORM_EPS) * gain


def _inproj_kernel(x_ref, g_ref, w_ref, o_ref, xn_ref):
    @pl.when(pl.program_id(1) == 0)
    def _():
        xn_ref[...] = _rms(x_ref[...], g_ref[...]).astype(BF16)

    o_ref[...] = jnp.dot(xn_ref[...], w_ref[...], preferred_element_type=F32).astype(o_ref.dtype)


def _inproj(x, gain, w):
    n = x.shape[0]
    tm = 1024
    return pl.pallas_call(
        _inproj_kernel,
        out_shape=jax.ShapeDtypeStruct((n, PROJ_W), BF16),
        grid=(n // tm, PROJ_W // PROJ_TN),
        in_specs=[
            pl.BlockSpec((tm, D_MODEL), lambda i, j: (i, 0)),
            pl.BlockSpec((1, D_MODEL), lambda i, j: (0, 0)),
            pl.BlockSpec((D_MODEL, PROJ_TN), lambda i, j: (0, j)),
        ],
        out_specs=pl.BlockSpec((tm, PROJ_TN), lambda i, j: (i, j)),
        scratch_shapes=[pltpu.VMEM((tm, D_MODEL), BF16)],
        compiler_params=_params(("parallel", "arbitrary"), BIG_VMEM_LIMIT),
        name="in_proj",
    )(x, gain, w)


def _prep_kernel(q_ref, kv_ref, rt_ref, ct_ref, qg_ref, kg_ref, qo_ref, ko_ref, vt_ref):
    rows_per_tile = rt_ref.shape[1]

    def table(c):
        rt = rt_ref[c]
        by_row = jnp.concatenate(
            [jnp.broadcast_to(rt[g:g + 1], (GRID_W, HEAD_DIM)) for g in range(rows_per_tile)],
            axis=0)
        return by_row + jnp.tile(ct_ref[c], (rows_per_tile, 1))

    cos, sa, sb = table(0), table(1), table(2)

    def norm_rope(x, gain):
        y = _rms(x.astype(F32), gain)
        return y * cos + pltpu.roll(y, 96, 1) * sa + pltpu.roll(y, 32, 1) * sb

    qg = qg_ref[...]
    kg = kg_ref[...]
    for h in range(N_Q_HEADS):
        sl = slice(h * HEAD_DIM, (h + 1) * HEAD_DIM)
        qo_ref[:, sl] = (norm_rope(q_ref[:, sl], qg) * (LOG2_E * HEAD_DIM ** -0.5)).astype(BF16)
    for h in range(N_KV_HEADS):
        sl = slice(h * HEAD_DIM, (h + 1) * HEAD_DIM)
        ko_ref[:, sl] = norm_rope(kv_ref[:, sl], kg).astype(BF16)
        v = kv_ref[:, KV_WIDTH + h * HEAD_DIM:KV_WIDTH + (h + 1) * HEAD_DIM]
        vt_ref[0, h, 0] = v.astype(F32).T.astype(BF16)


def _prep(proj, tables, q_gain, k_gain, b, t):
    n = b * t
    tm = ATT_TK
    nt = t // tm
    row_tab, col_tab = tables
    rows_per_tile = tm // GRID_W
    row_spec = pl.BlockSpec((3, rows_per_tile, HEAD_DIM), lambda i: (0, i % nt, 0))
    col_spec = pl.BlockSpec((3, GRID_W, HEAD_DIM), lambda i: (0, 0, 0))
    gain_spec = pl.BlockSpec((1, HEAD_DIM), lambda i: (0, 0))
    return pl.pallas_call(
        _prep_kernel,
        out_shape=(
            jax.ShapeDtypeStruct((n, ATTN_WIDTH), BF16),
            jax.ShapeDtypeStruct((n, KV_WIDTH), BF16),
            jax.ShapeDtypeStruct((b, N_KV_HEADS, nt, HEAD_DIM, tm), BF16),
        ),
        grid=(n // tm,),
        in_specs=[
            pl.BlockSpec((tm, ATTN_WIDTH), lambda i: (i, COL_QA // ATTN_WIDTH)),
            pl.BlockSpec((tm, 2 * KV_WIDTH), lambda i: (i, COL_KVA // (2 * KV_WIDTH))),
            row_spec, col_spec, gain_spec, gain_spec,
        ],
        out_specs=(
            pl.BlockSpec((tm, ATTN_WIDTH), lambda i: (i, 0)),
            pl.BlockSpec((tm, KV_WIDTH), lambda i: (i, 0)),
            pl.BlockSpec((1, N_KV_HEADS, 1, HEAD_DIM, tm), lambda i: (i // nt, 0, i % nt, 0, 0)),
        ),
        compiler_params=_params(("parallel",)),
        name="qkv_prep",
    )(proj, proj, row_tab, col_tab, q_gain, k_gain)


def _rope_tables(t):
    n_rows = t // GRID_W
    sec = HEAD_DIM // 2
    inv_freq = ROPE_THETA ** (-jnp.arange(0, sec, 2, dtype=F32) / sec)
    def planes(ang):
        c, s, z = jnp.cos(ang), jnp.sin(ang), jnp.zeros_like(ang)
        return jnp.stack([jnp.concatenate([c, c], 1), jnp.concatenate([-s, z], 1),
                          jnp.concatenate([z, s], 1)])

    row = planes(jnp.arange(n_rows, dtype=F32)[:, None] * inv_freq[None, :])
    col = planes(jnp.arange(GRID_W, dtype=F32)[:, None] * inv_freq[None, :])
    row_tab = jnp.concatenate([row, jnp.zeros_like(row)], axis=2)
    col_tab = jnp.concatenate([jnp.zeros_like(col), col], axis=2)
    return row_tab, col_tab


def _attn_kernel(bounded_ref, q_ref, k_ref, vt_ref, o_ref, acc_ref, l_ref, m_ref, *, nk):
    tq = q_ref.shape[0]
    tk = vt_ref.shape[-1]
    cols = Q_GROUP * tq
    sub = l_ref.shape[0]

    qt = jnp.concatenate(
        [q_ref[:, g * HEAD_DIM:(g + 1) * HEAD_DIM].astype(F32).T.astype(BF16)
         for g in range(Q_GROUP)], axis=1)

    def kv_tile(j):
        return k_ref[pl.ds(pl.multiple_of(j * tk, tk), tk), :], vt_ref[0, 0, j]

    def sublane_partial_sum(p):
        return jnp.sum(p.reshape(tk // sub, sub, cols), axis=0)

    acc_ref[...] = jnp.zeros_like(acc_ref)
    l_ref[...] = jnp.zeros_like(l_ref)

    @pl.when(bounded_ref[0] != 0)
    def _():
        group = min(ATT_KV_GROUP, nk)

        def body(jg, carry):
            l_sum = pv_sum = None
            for u in range(group):
                k, vt = kv_tile(jg * group + u)
                p = jnp.exp2(jnp.dot(k, qt, preferred_element_type=F32))
                l_u = sublane_partial_sum(p)
                pv_u = jnp.dot(vt, p.astype(BF16), preferred_element_type=F32)
                l_sum = l_u if l_sum is None else l_sum + l_u
                pv_sum = pv_u if pv_sum is None else pv_sum + pv_u
            l_ref[...] += l_sum
            acc_ref[...] += pv_sum
            return carry

        lax.fori_loop(0, nk // group, body, 0)

    @pl.when(bounded_ref[0] == 0)
    def _():
        m_ref[...] = jnp.full_like(m_ref, -jnp.inf)

        def body(j, carry):
            k, vt = kv_tile(j)
            s = jnp.dot(k, qt, preferred_element_type=F32)
            m_old = m_ref[...]
            m_new = jnp.maximum(m_old, jnp.max(s, axis=0, keepdims=True))
            alpha = jnp.exp2(m_old - m_new)
            p = jnp.exp2(s - m_new)
            m_ref[...] = m_new
            l_ref[...] = alpha * l_ref[...] + sublane_partial_sum(p)
            acc_ref[...] = alpha * acc_ref[...] + jnp.dot(vt, p.astype(BF16),
                                                          preferred_element_type=F32)
            return carry

        lax.fori_loop(0, nk, body, 0)

    out_t = acc_ref[...] * (1.0 / jnp.sum(l_ref[...], axis=0, keepdims=True))
    for g in range(Q_GROUP):
        o_ref[:, g * HEAD_DIM:(g + 1) * HEAD_DIM] = out_t[:, g * tq:(g + 1) * tq].T.astype(BF16)


def _scores_bounded(q_gain, k_gain):
    bound = (1.02 * LOG2_E * HEAD_DIM ** 0.5) * jnp.max(jnp.abs(q_gain)) * jnp.max(jnp.abs(k_gain))
    return (bound <= ATT_BOUNDED_MAX_LOG2).astype(jnp.int32).reshape(1)


def _attention(q, k, vt, bounded, b, t):
    n = b * t
    tq = ATT_TQ
    nq = t // tq
    nk = t // ATT_TK
    gw = Q_GROUP * HEAD_DIM
    return pl.pallas_call(
        functools.partial(_attn_kernel, nk=nk),
        out_shape=jax.ShapeDtypeStruct((n, ATTN_WIDTH), BF16),
        grid=(b, N_KV_HEADS, nq),
        in_specs=[
            pl.BlockSpec(memory_space=pltpu.SMEM),
            pl.BlockSpec((tq, gw), lambda bi, h, i: (bi * nq + i, h)),
            pl.BlockSpec((t, HEAD_DIM), lambda bi, h, i: (bi, h)),
            pl.BlockSpec((1, 1, nk, HEAD_DIM, ATT_TK), lambda bi, h, i: (bi, h, 0, 0, 0)),
        ],
        out_specs=pl.BlockSpec((tq, gw), lambda bi, h, i: (bi * nq + i, h)),
        scratch_shapes=[
            pltpu.VMEM((HEAD_DIM, Q_GROUP * tq), F32),
            pltpu.VMEM((SUBLANES, Q_GROUP * tq), F32),
            pltpu.VMEM((1, Q_GROUP * tq), F32),
        ],
        compiler_params=_params(("parallel", "parallel", "arbitrary")),
        name="attention",
    )(bounded, q, k, vt)


def _gla_dir(q_ref, k_ref, v_ref, lr_ref, w_ref, b_ref, tri_ref, mask_ref, o_ref, s_ref, cs_ref,
             reverse):
    c_sz = GLA_CHUNK
    n_chunks = GLA_TB // c_sz
    z = jnp.dot(lr_ref[...], w_ref[...], preferred_element_type=F32) + b_ref[...]
    yield
    softplus2 = jnp.log2(1.0 + jnp.exp2(jnp.abs(z) * (-LOG2_E)))
    la = (jnp.minimum(z, 0.0) * (LOG2_E / GLA_GATE_NORMALIZER)
          - softplus2 * (1.0 / GLA_GATE_NORMALIZER))
    hi = la.astype(BF16)
    lo = (la - hi.astype(F32)).astype(BF16)
    tri = tri_ref[...]
    cs = (jnp.dot(tri, hi, preferred_element_type=F32)
          + jnp.dot(tri, lo, preferred_element_type=F32))
    yield
    for h in range(GLA_HEADS):
        cs_ref[h] = cs[:, h * GLA_DK:(h + 1) * GLA_DK]
    half = c_sz // 2
    edge_start = 0 if reverse else half - 1
    edge_row = [2 * c if reverse else 2 * c + 1 for c in range(n_chunks)]

    qe = (q_ref[...].astype(F32) * (GLA_DK ** -0.5)) * jnp.exp2(cs)
    ke = k_ref[...].astype(F32) * jnp.exp2(-cs)
    qe_b = qe.astype(BF16)
    ke_b = ke.astype(BF16)
    v_b = v_ref[...]
    mask = mask_ref[...] > 0.5
    chunk_of_lane = lax.broadcasted_iota(jnp.int32, (GLA_DK, GLA_TB), 1) // c_sz
    order = range(n_chunks - 1, -1, -1) if reverse else range(n_chunks)

    heads = range(GLA_HEADS)
    ks = [slice(h * GLA_DK, (h + 1) * GLA_DK) for h in heads]
    vs = [slice(h * GLA_DV, (h + 1) * GLA_DV) for h in heads]
    yield

    a = [jnp.where(mask, lax.dot_general(qe_b[:, ks[h]], ke_b[:, ks[h]], (((1,), (1,)), ((), ())),
                                         preferred_element_type=F32), 0.0).astype(BF16)
         for h in heads]
    yield

    decay_t, prod = [], []

    def recurrence(h):
        s = s_ref[h]
        for c in order:
            rs = slice(c * c_sz, (c + 1) * c_sz)
            r = edge_row[c]
            o_inter = jnp.dot(qe_b[rs, ks[h]], s.astype(BF16), preferred_element_type=F32)
            o_ref[rs, vs[h]] = prod[h][rs] + o_inter
            inc = prod[h][GLA_TB + c * GLA_DK:GLA_TB + (c + 1) * GLA_DK]
            s = decay_t[h][:, r:r + 1] * s + inc
        s_ref[h] = s

    for h in heads:
        edge = jnp.exp2(cs_ref[h, pl.ds(edge_start, 2 * n_chunks, stride=half), :])
        decay_t.append(jnp.tile(edge, (GLA_DK // (2 * n_chunks), 1)).T)
        kd = ke[:, ks[h]] * jnp.concatenate(
            [jnp.broadcast_to(edge[r:r + 1], (c_sz, GLA_DK)) for r in edge_row], axis=0)
        kd_t = kd.T
        lhs = jnp.concatenate(
            [a[h]]
            + [jnp.where(chunk_of_lane == c, kd_t, 0.0).astype(BF16) for c in range(n_chunks)],
            axis=0)
        prod.append(jnp.dot(lhs, v_b[:, vs[h]], preferred_element_type=F32))
        yield

    for h in heads:
        recurrence(h)
        yield


def _gla_kernel(qf_ref, kf_ref, vf_ref, lrf_ref, qb_ref, kb_ref, vb_ref, lrb_ref,
                wf_ref, bf_ref, wb_ref, bb_ref, trif_ref, trib_ref, maskf_ref, maskb_ref,
                of_ref, ob_ref, sf_ref, sb_ref, csf_ref, csb_ref):
    @pl.when(pl.program_id(1) == 0)
    def _():
        sf_ref[...] = jnp.zeros_like(sf_ref)
        sb_ref[...] = jnp.zeros_like(sb_ref)

    fwd = _gla_dir(qf_ref, kf_ref, vf_ref, lrf_ref, wf_ref, bf_ref, trif_ref, maskf_ref, of_ref,
                   sf_ref, csf_ref, reverse=False)
    bwd = _gla_dir(qb_ref, kb_ref, vb_ref, lrb_ref, wb_ref, bb_ref, trib_ref, maskb_ref, ob_ref,
                   sb_ref, csb_ref, reverse=True)
    for _ in itertools.zip_longest(fwd, bwd):
        pass


def _gla_consts():
    i = jnp.arange(GLA_TB)[:, None]
    j = jnp.arange(GLA_TB)[None, :]
    same = (i // GLA_CHUNK) == (j // GLA_CHUNK)
    tri_f = (same & (j <= i)).astype(F32)
    tri_b = (same & (j >= i)).astype(F32)
    mask_b = (same & (j > i)).astype(F32)
    return tri_f, tri_b, mask_b


def _gla(proj, w_f, b_f, w_b, b_b, b, t):
    n = b * t
    tb = GLA_TB
    nt = t // tb
    tri_f, tri_b, mask_b = _gla_consts()

    def fwd(width, col):
        return pl.BlockSpec((tb, width), lambda bi, i: (bi * nt + i, col // width))

    def bwd(width, col):
        return pl.BlockSpec((tb, width), lambda bi, i: (bi * nt + nt - 1 - i, col // width))

    def const(shape):
        return pl.BlockSpec(shape, lambda bi, i: (0, 0))

    pieces = ((GLA_K_WIDTH, COL_QG), (GLA_K_WIDTH, COL_KG), (GLA_V_WIDTH, COL_VG), (LANES, COL_LR))
    out_sd = jax.ShapeDtypeStruct((n, GLA_V_WIDTH), F32)
    return pl.pallas_call(
        _gla_kernel,
        out_shape=(out_sd, out_sd),
        grid=(b, nt),
        in_specs=[fwd(*p) for p in pieces] + [bwd(*p) for p in pieces] + [
            const((LANES, GLA_K_WIDTH)), const((1, GLA_K_WIDTH)),
            const((LANES, GLA_K_WIDTH)), const((1, GLA_K_WIDTH)),
            const((tb, tb)), const((tb, tb)), const((tb, tb)), const((tb, tb)),
        ],
        out_specs=(
            pl.BlockSpec((tb, GLA_V_WIDTH), lambda bi, i: (bi * nt + i, 0)),
            pl.BlockSpec((tb, GLA_V_WIDTH), lambda bi, i: (bi * nt + nt - 1 - i, 0)),
        ),
        scratch_shapes=[
            pltpu.VMEM((GLA_HEADS, GLA_DK, GLA_DV), F32),
            pltpu.VMEM((GLA_HEADS, GLA_DK, GLA_DV), F32),
            pltpu.VMEM((GLA_HEADS, tb, GLA_DK), F32),
            pltpu.VMEM((GLA_HEADS, tb, GLA_DK), F32),
        ],
        compiler_params=_params(("parallel", "arbitrary")),
        name="gla",
    )(*([proj] * 8), w_f, b_f, w_b, b_b, tri_f.astype(BF16), tri_b.astype(BF16), tri_f, mask_b)


def _merge_kernel(att_ref, of_ref, ob_ref, g_ref, gla_ref, glb_ref, wa_ref, wg_ref, gn_ref,
                  bma_ref, bmb_ref, o_ref):
    blocks = [slice(c * MERGE_TN, (c + 1) * MERGE_TN) for c in range(D_MODEL // MERGE_TN)]
    att = att_ref[...]
    ga_aa = [_sigmoid(gla_ref[:, cs].astype(F32) + bma_ref[:, cs])
             * jnp.dot(att, wa_ref[:, cs], preferred_element_type=F32) for cs in blocks]
    o = of_ref[...] + ob_ref[...]
    gn = gn_ref[...]
    on = jnp.concatenate(
        [_rms(o[:, h * GLA_DV:(h + 1) * GLA_DV], gn) for h in range(GLA_HEADS)], axis=1)
    g = g_ref[...].astype(F32)
    gl = (on * (g * _sigmoid(g))).astype(BF16)
    for cs, a_part in zip(blocks, ga_aa):
        bb = jnp.dot(gl, wg_ref[:, cs], preferred_element_type=F32)
        gb = _sigmoid(glb_ref[:, cs].astype(F32) + bmb_ref[:, cs])
        o_ref[:, cs] = (a_part + gb * bb).astype(BF16)


def _merge(att, o_f, o_b, proj, w_attn, w_gla, gla_norm, b_merge):
    n = att.shape[0]
    tm = 512
    row = lambda w, c=0: pl.BlockSpec((tm, w), lambda i: (i, c))
    const = lambda r, w, c=0: pl.BlockSpec((r, w), lambda i: (0, c))
    resident = lambda r, w: pl.BlockSpec((r, w), lambda i: (0, 0), pipeline_mode=pl.Buffered(1))
    return pl.pallas_call(
        _merge_kernel,
        out_shape=jax.ShapeDtypeStruct((n, D_MODEL), BF16),
        grid=(n // tm,),
        in_specs=[
            row(ATTN_WIDTH), row(GLA_V_WIDTH), row(GLA_V_WIDTH),
            row(GLA_V_WIDTH, COL_GG // GLA_V_WIDTH),
            row(D_MODEL, 0), row(D_MODEL, 1),
            resident(ATTN_WIDTH, D_MODEL), resident(GLA_V_WIDTH, D_MODEL), const(1, GLA_DV),
            const(1, D_MODEL, 0), const(1, D_MODEL, 1),
        ],
        out_specs=row(D_MODEL),
        compiler_params=_params(("parallel",)),
        name="merge",
    )(att, o_f, o_b, proj, proj, proj, w_attn, w_gla, gla_norm, b_merge, b_merge)


def _outproj_kernel(x_ref, m_ref, w_ref, g_ref, h_ref, hn_ref):
    h = x_ref[...] + jnp.dot(m_ref[...], w_ref[...], preferred_element_type=F32)
    h_ref[...] = h
    hn_ref[...] = _rms(h, g_ref[...]).astype(BF16)


def _outproj(x, mixed, w_out, gain_mlp):
    n = x.shape[0]
    tm = 512
    row = pl.BlockSpec((tm, D_MODEL), lambda i: (i, 0))
    return pl.pallas_call(
        _outproj_kernel,
        out_shape=(jax.ShapeDtypeStruct((n, D_MODEL), F32),
                   jax.ShapeDtypeStruct((n, D_MODEL), BF16)),
        grid=(n // tm,),
        in_specs=[row, row, pl.BlockSpec((D_MODEL, D_MODEL), lambda i: (0, 0)),
                  pl.BlockSpec((1, D_MODEL), lambda i: (0, 0))],
        out_specs=(row, row),
        compiler_params=_params(("parallel",)),
        name="out_proj",
    )(x, mixed, w_out, gain_mlp)


def _mlp_kernel(h_ref, hn_ref, wu_ref, wd_ref, gf_ref, o_ref, *, final_norm):
    j = pl.program_id(1)
    last = pl.num_programs(1) - 1

    def down():
        u = jnp.maximum(jnp.dot(hn_ref[...], wu_ref[...], preferred_element_type=F32), 0.0)
        return jnp.dot((u * u).astype(BF16), wd_ref[...], preferred_element_type=F32)

    @pl.when(j == 0)
    def _():
        o_ref[...] = h_ref[...] + down()

    @pl.when(jnp.logical_and(j > 0, j < last))
    def _():
        o_ref[...] += down()

    @pl.when(j == last)
    def _():
        o = o_ref[...] + down()
        o_ref[...] = _rms(o, gf_ref[...]) if final_norm else o


def _mlp(h, hn, w_up, w_down, gain_final, final_norm):
    n = h.shape[0]
    tm = 512
    tf = 2048
    assert D_FF // tf >= 2
    row = pl.BlockSpec((tm, D_MODEL), lambda i, j: (i, 0))
    vec = pl.BlockSpec((1, D_MODEL), lambda i, j: (0, 0))
    return pl.pallas_call(
        functools.partial(_mlp_kernel, final_norm=final_norm),
        out_shape=jax.ShapeDtypeStruct((n, D_MODEL), F32),
        grid=(n // tm, D_FF // tf),
        in_specs=[
            row, row,
            pl.BlockSpec((D_MODEL, tf), lambda i, j: (0, j)),
            pl.BlockSpec((tf, D_MODEL), lambda i, j: (j, 0)),
            vec,
        ],
        out_specs=row,
        compiler_params=_params(("parallel", "arbitrary"), BIG_VMEM_LIMIT),
        name="mlp",
    )(h, hn, w_up, w_down, gain_final)


_SRC_WIDTHS = (ATTN_WIDTH, KV_WIDTH, KV_WIDTH, GLA_K_WIDTH, GLA_K_WIDTH, GLA_V_WIDTH, GLA_V_WIDTH,
               GLA_GATE_RANK, GLA_GATE_RANK, N_BRANCHES * D_MODEL)
_SRC_OFF = tuple(sum(_SRC_WIDTHS[:i]) for i in range(len(_SRC_WIDTHS) + 1))
D_IN_PROJ = _SRC_OFF[-1]
_PROJ_MOVES = ((COL_QA, 0), (COL_KVA, 1), (COL_KVA + KV_WIDTH, 2), (COL_QG, 3), (COL_KG, 4),
               (COL_VG, 5), (COL_GG, 6))


def _wproj_kernel(wt_ref, o_ref):
    def put(dst, src, width):
        o_ref[:, dst:dst + width] = wt_ref[src:src + width, :].T.astype(BF16)

    put(COL_GATE, _SRC_OFF[9], N_BRANCHES * D_MODEL)
    for dst, piece in _PROJ_MOVES:
        put(dst, _SRC_OFF[piece], _SRC_WIDTHS[piece])
    lr = wt_ref[_SRC_OFF[7]:_SRC_OFF[7] + LANES, :].T
    lane = lax.broadcasted_iota(jnp.int32, lr.shape, 1)
    o_ref[:, COL_LR:COL_LR + LANES] = jnp.where(lane < 2 * GLA_GATE_RANK, lr, 0.0).astype(BF16)
    o_ref[:, COL_LR + LANES:] = jnp.zeros((o_ref.shape[0], PROJ_W - COL_LR - LANES), BF16)


def _wproj(w_in):
    wt = jnp.swapaxes(w_in, 0, 1)
    return pl.pallas_call(
        _wproj_kernel,
        out_shape=jax.ShapeDtypeStruct((D_MODEL, PROJ_W), BF16),
        grid=(D_MODEL // LANES,),
        in_specs=[pl.BlockSpec((D_IN_PROJ, LANES), lambda i: (0, i))],
        out_specs=pl.BlockSpec((LANES, PROJ_W), lambda i: (i, 0)),
        compiler_params=_params(("parallel",)),
        name="w_proj_layout",
    )(wt)


def _prep_layer(w_in, w_gate_up_fwd, b_gate_fwd, w_gate_up_bwd, b_gate_bwd, w_attn_proj,
                w_gla_proj, w_out, w_up, w_down):
    w_proj = _wproj(w_in)
    zf = jnp.zeros((LANES - GLA_GATE_RANK, GLA_K_WIDTH), F32)
    zb = jnp.zeros((LANES - 2 * GLA_GATE_RANK, GLA_K_WIDTH), F32)
    w_f = jnp.concatenate([w_gate_up_fwd, zf], axis=0).astype(BF16)
    w_b = jnp.concatenate([jnp.zeros((GLA_GATE_RANK, GLA_K_WIDTH), F32), w_gate_up_bwd, zb],
                          axis=0).astype(BF16)
    return dict(
        w_proj=w_proj, w_f=w_f, w_b=w_b,
        b_f=b_gate_fwd.reshape(1, -1), b_b=b_gate_bwd.reshape(1, -1),
        w_attn=w_attn_proj.astype(BF16), w_gla=w_gla_proj.astype(BF16),
        w_out=w_out.astype(BF16), w_up=w_up.astype(BF16), w_down=w_down.astype(BF16))


def _layer(x, b, t, lw, norm_mix, q_norm, k_norm, gla_norm, b_merge, norm_mlp, norm_final,
           final_norm):
    proj = _inproj(x, norm_mix.reshape(1, -1), lw["w_proj"])
    q, k, vt = _prep(proj, _rope_tables(t), q_norm.reshape(1, -1), k_norm.reshape(1, -1), b, t)
    att = _attention(q, k, vt, _scores_bounded(q_norm, k_norm), b, t)
    o_f, o_b = _gla(proj, lw["w_f"], lw["b_f"], lw["w_b"], lw["b_b"], b, t)
    mixed = _merge(att, o_f, o_b, proj, lw["w_attn"], lw["w_gla"], gla_norm.reshape(1, -1),
                   b_merge.reshape(1, -1))
    h, hn = _outproj(x, mixed, lw["w_out"], norm_mlp.reshape(1, -1))
    return _mlp(h, hn, lw["w_up"], lw["w_down"], norm_final.reshape(1, -1), final_norm)


def kernel(x_prompt, x_sample, norm_mix, w_in, q_norm, k_norm, w_gate_up_fwd, b_gate_fwd,
           w_gate_up_bwd, b_gate_bwd, gla_norm, w_attn_proj, w_gla_proj, b_merge, w_out,
           norm_mlp, w_up, w_down, norm_final):
    layers = [
        _prep_layer(w_in[l], w_gate_up_fwd[l], b_gate_fwd[l], w_gate_up_bwd[l], b_gate_bwd[l],
                    w_attn_proj[l], w_gla_proj[l], w_out[l], w_up[l], w_down[l])
        for l in range(DEPTH)]

    def trunk(x):
        b, t, d = x.shape
        y = x.reshape(b * t, d)
        for l in range(DEPTH):
            y = _layer(y, b, t, layers[l], norm_mix[l], q_norm[l], k_norm[l], gla_norm[l],
                       b_merge[l], norm_mlp[l], norm_final, final_norm=(l == DEPTH - 1))
        return y.reshape(b, t, d)

    return trunk(x_prompt), trunk(x_sample)
```

```python
import functools
import itertools

import jax
import jax.numpy as jnp
from jax import lax
from jax.experimental import pallas as pl
from jax.experimental.pallas import tpu as pltpu

F32 = jnp.float32
BF16 = jnp.bfloat16

D_MODEL = 2048
DEPTH = 1
GRID_W = 64
HEAD_DIM = 128
N_Q_HEADS = 8
N_KV_HEADS = 2
Q_GROUP = N_Q_HEADS // N_KV_HEADS
ATTN_WIDTH = N_Q_HEADS * HEAD_DIM
KV_WIDTH = N_KV_HEADS * HEAD_DIM
ROPE_THETA = 10000.0
GLA_HEADS = 4
GLA_DK = 128
GLA_DV = 256
GLA_K_WIDTH = GLA_HEADS * GLA_DK
GLA_V_WIDTH = GLA_HEADS * GLA_DV
GLA_GATE_RANK = 16
GLA_GATE_NORMALIZER = 16.0
GLA_CHUNK = 64
N_BRANCHES = 2
D_FF = 4 * D_MODEL
NORM_EPS = 1e-6

LANES = 128
SUBLANES = 8
VMEM_LIMIT = 48 * 1024 * 1024
BIG_VMEM_LIMIT = 58 * 1024 * 1024

COL_GATE = 0
COL_QA = COL_GATE + N_BRANCHES * D_MODEL
COL_VG = COL_QA + ATTN_WIDTH
COL_GG = COL_VG + GLA_V_WIDTH
COL_KVA = COL_GG + GLA_V_WIDTH
COL_QG = COL_KVA + 2 * KV_WIDTH
COL_KG = COL_QG + GLA_K_WIDTH
COL_LR = COL_KG + GLA_K_WIDTH
PROJ_TN = 1792
PROJ_W = 5 * PROJ_TN

GLA_TB = 256
GLA_SUB = 2
MERGE_TN = D_MODEL // 2
ATT_TQ = 256
ATT_TK = 2048
ATT_KV_GROUP = 2
LOG2_E = 1.4426950408889634
ATT_BOUNDED_MAX_LOG2 = 60.0


def _params(sem, vmem_limit=VMEM_LIMIT):
    return pltpu.CompilerParams(dimension_semantics=sem, vmem_limit_bytes=vmem_limit)


def _sigmoid(x):
    return 1.0 / (1.0 + jnp.exp(-x))


def _rms(x, gain):
    ms = jnp.mean(x * x, axis=-1, keepdims=True)
    return x * lax.rsqrt(ms + NORM_EPS) * gain


def _inproj_kernel(x_ref, g_ref, w_ref, o_ref, xn_ref):
    @pl.when(pl.program_id(1) == 0)
    def _():
        xn_ref[...] = _rms(x_ref[...], g_ref[...]).astype(BF16)

    o_ref[...] = jnp.dot(xn_ref[...], w_ref[...], preferred_element_type=F32).astype(o_ref.dtype)


def _inproj(x, gain, w):
    n = x.shape[0]
    tm = 1024
    return pl.pallas_call(
        _inproj_kernel,
        out_shape=jax.ShapeDtypeStruct((n, PROJ_W), BF16),
        grid=(n // tm, PROJ_W // PROJ_TN),
        in_specs=[
            pl.BlockSpec((tm, D_MODEL), lambda i, j: (i, 0)),
            pl.BlockSpec((1, D_MODEL), lambda i, j: (0, 0)),
            pl.BlockSpec((D_MODEL, PROJ_TN), lambda i, j: (0, j)),
        ],
        out_specs=pl.BlockSpec((tm, PROJ_TN), lambda i, j: (i, j)),
        scratch_shapes=[pltpu.VMEM((tm, D_MODEL), BF16)],
        compiler_params=_params(("parallel", "arbitrary"), BIG_VMEM_LIMIT),
        name="in_proj",
    )(x, gain, w)


def _prep_kernel(q_ref, kv_ref, rt_ref, ct_ref, qg_ref, kg_ref, qo_ref, ko_ref, vt_ref):
    rows_per_tile = rt_ref.shape[1]

    def table(c):
        rt = rt_ref[c]
        by_row = jnp.concatenate(
            [jnp.broadcast_to(rt[g:g + 1], (GRID_W, HEAD_DIM)) for g in range(rows_per_tile)],
            axis=0)
        return by_row + jnp.tile(ct_ref[c], (rows_per_tile, 1))

    cos, sa, sb = table(0), table(1), table(2)

    def norm_rope(x, gain):
        y = _rms(x.astype(F32), gain)
        return y * cos + pltpu.roll(y, 96, 1) * sa + pltpu.roll(y, 32, 1) * sb

    qg = qg_ref[...]
    kg = kg_ref[...]
    for h in range(N_Q_HEADS):
        sl = slice(h * HEAD_DIM, (h + 1) * HEAD_DIM)
        qo_ref[:, sl] = (norm_rope(q_ref[:, sl], qg) * (LOG2_E * HEAD_DIM ** -0.5)).astype(BF16)
    for h in range(N_KV_HEADS):
        sl = slice(h * HEAD_DIM, (h + 1) * HEAD_DIM)
        ko_ref[:, sl] = norm_rope(kv_ref[:, sl], kg).astype(BF16)
        v = kv_ref[:, KV_WIDTH + h * HEAD_DIM:KV_WIDTH + (h + 1) * HEAD_DIM]
        vt_ref[0, h, 0] = v.astype(F32).T.astype(BF16)


def _prep(proj, tables, q_gain, k_gain, b, t):
    n = b * t
    tm = ATT_TK
    nt = t // tm
    row_tab, col_tab = tables
    rows_per_tile = tm // GRID_W
    row_spec = pl.BlockSpec((3, rows_per_tile, HEAD_DIM), lambda i: (0, i % nt, 0))
    col_spec = pl.BlockSpec((3, GRID_W, HEAD_DIM), lambda i: (0, 0, 0))
    gain_spec = pl.BlockSpec((1, HEAD_DIM), lambda i: (0, 0))
    return pl.pallas_call(
        _prep_kernel,
        out_shape=(
            jax.ShapeDtypeStruct((n, ATTN_WIDTH), BF16),
            jax.ShapeDtypeStruct((n, KV_WIDTH), BF16),
            jax.ShapeDtypeStruct((b, N_KV_HEADS, nt, HEAD_DIM, tm), BF16),
        ),
        grid=(n // tm,),
        in_specs=[
            pl.BlockSpec((tm, ATTN_WIDTH), lambda i: (i, COL_QA // ATTN_WIDTH)),
            pl.BlockSpec((tm, 2 * KV_WIDTH), lambda i: (i, COL_KVA // (2 * KV_WIDTH))),
            row_spec, col_spec, gain_spec, gain_spec,
        ],
        out_specs=(
            pl.BlockSpec((tm, ATTN_WIDTH), lambda i: (i, 0)),
            pl.BlockSpec((tm, KV_WIDTH), lambda i: (i, 0)),
            pl.BlockSpec((1, N_KV_HEADS, 1, HEAD_DIM, tm), lambda i: (i // nt, 0, i % nt, 0, 0)),
        ),
        compiler_params=_params(("parallel",)),
        name="qkv_prep",
    )(proj, proj, row_tab, col_tab, q_gain, k_gain)


def _rope_tables(t):
    n_rows = t // GRID_W
    sec = HEAD_DIM // 2
    inv_freq = ROPE_THETA ** (-jnp.arange(0, sec, 2, dtype=F32) / sec)
    def planes(ang):
        c, s, z = jnp.cos(ang), jnp.sin(ang), jnp.zeros_like(ang)
        return jnp.stack([jnp.concatenate([c, c], 1), jnp.concatenate([-s, z], 1),
                          jnp.concatenate([z, s], 1)])

    row = planes(jnp.arange(n_rows, dtype=F32)[:, None] * inv_freq[None, :])
    col = planes(jnp.arange(GRID_W, dtype=F32)[:, None] * inv_freq[None, :])
    row_tab = jnp.concatenate([row, jnp.zeros_like(row)], axis=2)
    col_tab = jnp.concatenate([jnp.zeros_like(col), col], axis=2)
    return row_tab, col_tab


def _attn_kernel(bounded_ref, q_ref, k_ref, vt_ref, o_ref, acc_ref, l_ref, m_ref, *, nk):
    tq = q_ref.shape[0]
    tk = vt_ref.shape[-1]
    cols = Q_GROUP * tq
    sub = l_ref.shape[0]

    qt = jnp.concatenate(
        [q_ref[:, g * HEAD_DIM:(g + 1) * HEAD_DIM].astype(F32).T.astype(BF16)
         for g in range(Q_GROUP)], axis=1)

    def kv_tile(j):
        return k_ref[pl.ds(pl.multiple_of(j * tk, tk), tk), :], vt_ref[0, 0, j]

    def sublane_partial_sum(p):
        return jnp.sum(p.reshape(tk // sub, sub, cols), axis=0)

    acc_ref[...] = jnp.zeros_like(acc_ref)
    l_ref[...] = jnp.zeros_like(l_ref)

    @pl.when(bounded_ref[0] != 0)
    def _():
        group = min(ATT_KV_GROUP, nk)

        def body(jg, carry):
            l_sum = pv_sum = None
            for u in range(group):
                k, vt = kv_tile(jg * group + u)
                p = jnp.exp2(jnp.dot(k, qt, preferred_element_type=F32))
                l_u = sublane_partial_sum(p)
                pv_u = jnp.dot(vt, p.astype(BF16), preferred_element_type=F32)
                l_sum = l_u if l_sum is None else l_sum + l_u
                pv_sum = pv_u if pv_sum is None else pv_sum + pv_u
            l_ref[...] += l_sum
            acc_ref[...] += pv_sum
            return carry

        lax.fori_loop(0, nk // group, body, 0)

    @pl.when(bounded_ref[0] == 0)
    def _():
        m_ref[...] = jnp.full_like(m_ref, -jnp.inf)

        def body(j, carry):
            k, vt = kv_tile(j)
            s = jnp.dot(k, qt, preferred_element_type=F32)
            m_old = m_ref[...]
            m_new = jnp.maximum(m_old, jnp.max(s, axis=0, keepdims=True))
            alpha = jnp.exp2(m_old - m_new)
            p = jnp.exp2(s - m_new)
            m_ref[...] = m_new
            l_ref[...] = alpha * l_ref[...] + sublane_partial_sum(p)
            acc_ref[...] = alpha * acc_ref[...] + jnp.dot(vt, p.astype(BF16),
                                                          preferred_element_type=F32)
            return carry

        lax.fori_loop(0, nk, body, 0)

    out_t = acc_ref[...] * (1.0 / jnp.sum(l_ref[...], axis=0, keepdims=True))
    for g in range(Q_GROUP):
        o_ref[:, g * HEAD_DIM:(g + 1) * HEAD_DIM] = out_t[:, g * tq:(g + 1) * tq].T.astype(BF16)


def _scores_bounded(q_gain, k_gain):
    bound = (1.02 * LOG2_E * HEAD_DIM ** 0.5) * jnp.max(jnp.abs(q_gain)) * jnp.max(jnp.abs(k_gain))
    return (bound <= ATT_BOUNDED_MAX_LOG2).astype(jnp.int32).reshape(1)


def _attention(q, k, vt, bounded, b, t):
    n = b * t
    tq = ATT_TQ
    nq = t // tq
    nk = t // ATT_TK
    gw = Q_GROUP * HEAD_DIM
    return pl.pallas_call(
        functools.partial(_attn_kernel, nk=nk),
        out_shape=jax.ShapeDtypeStruct((n, ATTN_WIDTH), BF16),
        grid=(b, N_KV_HEADS, nq),
        in_specs=[
            pl.BlockSpec(memory_space=pltpu.SMEM),
            pl.BlockSpec((tq, gw), lambda bi, h, i: (bi * nq + i, h)),
            pl.BlockSpec((t, HEAD_DIM), lambda bi, h, i: (bi, h)),
            pl.BlockSpec((1, 1, nk, HEAD_DIM, ATT_TK), lambda bi, h, i: (bi, h, 0, 0, 0)),
        ],
        out_specs=pl.BlockSpec((tq, gw), lambda bi, h, i: (bi * nq + i, h)),
        scratch_shapes=[
            pltpu.VMEM((HEAD_DIM, Q_GROUP * tq), F32),
            pltpu.VMEM((SUBLANES, Q_GROUP * tq), F32),
            pltpu.VMEM((1, Q_GROUP * tq), F32),
        ],
        compiler_params=_params(("parallel", "parallel", "arbitrary")),
        name="attention",
    )(bounded, q, k, vt)


def _gla_dir(q_ref, k_ref, v_ref, lr_ref, w_ref, b_ref, tri_ref, mask_ref, o_ref, s_ref, cs_ref,
             reverse):
    c_sz = GLA_CHUNK
    n_chunks = GLA_TB // c_sz
    z = jnp.dot(lr_ref[...], w_ref[...], preferred_element_type=F32) + b_ref[...]
    yield
    softplus2 = jnp.log2(1.0 + jnp.exp2(jnp.abs(z) * (-LOG2_E)))
    la = (jnp.minimum(z, 0.0) * (LOG2_E / GLA_GATE_NORMALIZER)
          - softplus2 * (1.0 / GLA_GATE_NORMALIZER))
    hi = la.astype(BF16)
    lo = (la - hi.astype(F32)).astype(BF16)
    tri = tri_ref[...]
    cs = (jnp.dot(tri, hi, preferred_element_type=F32)
          + jnp.dot(tri, lo, preferred_element_type=F32))
    yield
    for h in range(GLA_HEADS):
        cs_ref[h] = cs[:, h * GLA_DK:(h + 1) * GLA_DK]
    half = c_sz // 2
    edge_start = 0 if reverse else half - 1
    edge_row = [2 * c if reverse else 2 * c + 1 for c in range(n_chunks)]

    qe = (q_ref[...].astype(F32) * (GLA_DK ** -0.5)) * jnp.exp2(cs)
    ke = k_ref[...].astype(F32) * jnp.exp2(-cs)
    qe_b = qe.astype(BF16)
    ke_b = ke.astype(BF16)
    v_b = v_ref[...]
    mask = mask_ref[...] > 0.5
    chunk_of_lane = lax.broadcasted_iota(jnp.int32, (GLA_DK, GLA_TB), 1) // c_sz
    order = range(n_chunks - 1, -1, -1) if reverse else range(n_chunks)

    heads = range(GLA_HEADS)
    ks = [slice(h * GLA_DK, (h + 1) * GLA_DK) for h in heads]
    vs = [slice(h * GLA_DV, (h + 1) * GLA_DV) for h in heads]
    yield

    a = [jnp.where(mask, lax.dot_general(qe_b[:, ks[h]], ke_b[:, ks[h]], (((1,), (1,)), ((), ())),
                                         preferred_element_type=F32), 0.0).astype(BF16)
         for h in heads]
    yield

    decay_t, prod = [], []

    def recurrence(h):
        s = s_ref[h]
        for c in order:
            rs = slice(c * c_sz, (c + 1) * c_sz)
            r = edge_row[c]
            o_inter = jnp.dot(qe_b[rs, ks[h]], s.astype(BF16), preferred_element_type=F32)
            o_ref[rs, vs[h]] = prod[h][rs] + o_inter
            inc = prod[h][GLA_TB + c * GLA_DK:GLA_TB + (c + 1) * GLA_DK]
            s = decay_t[h][:, r:r + 1] * s + inc
        s_ref[h] = s

    for h in heads:
        edge = jnp.exp2(cs_ref[h, pl.ds(edge_start, 2 * n_chunks, stride=half), :])
        decay_t.append(jnp.tile(edge, (GLA_DK // (2 * n_chunks), 1)).T)
        kd = ke[:, ks[h]] * jnp.concatenate(
            [jnp.broadcast_to(edge[r:r + 1], (c_sz, GLA_DK)) for r in edge_row], axis=0)
        kd_t = kd.T
        lhs = jnp.concatenate(
            [a[h]]
            + [jnp.where(chunk_of_lane == c, kd_t, 0.0).astype(BF16) for c in range(n_chunks)],
            axis=0)
        prod.append(jnp.dot(lhs, v_b[:, vs[h]], preferred_element_type=F32))
        yield

    for h in heads:
        recurrence(h)
        yield


def _gla_kernel(qf_ref, kf_ref, vf_ref, lrf_ref, qb_ref, kb_ref, vb_ref, lrb_ref,
                wf_ref, bf_ref, wb_ref, bb_ref, trif_ref, trib_ref, maskf_ref, maskb_ref,
                of_ref, ob_ref, sf_ref, sb_ref, *cs_refs):
    @pl.when(pl.program_id(1) == 0)
    def _():
        sf_ref[...] = jnp.zeros_like(sf_ref)
        sb_ref[...] = jnp.zeros_like(sb_ref)

    def rows(ref, sub):
        return ref.at[pl.ds(sub * GLA_TB, GLA_TB)]

    def fwd(sub, cs_ref):
        return _gla_dir(rows(qf_ref, sub), rows(kf_ref, sub), rows(vf_ref, sub), rows(lrf_ref, sub),
                        wf_ref, bf_ref, trif_ref, maskf_ref, rows(of_ref, sub), sf_ref, cs_ref,
                        reverse=False)

    def bwd(sub, cs_ref):
        return _gla_dir(rows(qb_ref, sub), rows(kb_ref, sub), rows(vb_ref, sub), rows(lrb_ref, sub),
                        wb_ref, bb_ref, trib_ref, maskb_ref, rows(ob_ref, sub), sb_ref, cs_ref,
                        reverse=True)

    streams = []
    for s in range(GLA_SUB):
        streams.append(fwd(s, cs_refs[2 * s]))
        streams.append(bwd(GLA_SUB - 1 - s, cs_refs[2 * s + 1]))
    for _ in itertools.zip_longest(*streams):
        pass


def _gla_consts():
    i = jnp.arange(GLA_TB)[:, None]
    j = jnp.arange(GLA_TB)[None, :]
    same = (i // GLA_CHUNK) == (j // GLA_CHUNK)
    tri_f = (same & (j <= i)).astype(F32)
    tri_b = (same & (j >= i)).astype(F32)
    mask_b = (same & (j > i)).astype(F32)
    return tri_f, tri_b, mask_b


def _gla(proj, w_f, b_f, w_b, b_b, b, t):
    n = b * t
    tb = GLA_TB
    step = GLA_SUB * tb
    nt = t // step
    tri_f, tri_b, mask_b = _gla_consts()

    def fwd(width, col):
        return pl.BlockSpec((step, width), lambda bi, i: (bi * nt + i, col // width))

    def bwd(width, col):
        return pl.BlockSpec((step, width), lambda bi, i: (bi * nt + nt - 1 - i, col // width))

    def const(shape):
        return pl.BlockSpec(shape, lambda bi, i: (0, 0))

    pieces = ((GLA_K_WIDTH, COL_QG), (GLA_K_WIDTH, COL_KG), (GLA_V_WIDTH, COL_VG), (LANES, COL_LR))
    out_sd = jax.ShapeDtypeStruct((n, GLA_V_WIDTH), F32)
    return pl.pallas_call(
        _gla_kernel,
        out_shape=(out_sd, out_sd),
        grid=(b, nt),
        in_specs=[fwd(*p) for p in pieces] + [bwd(*p) for p in pieces] + [
            const((LANES, GLA_K_WIDTH)), const((1, GLA_K_WIDTH)),
            const((LANES, GLA_K_WIDTH)), const((1, GLA_K_WIDTH)),
            const((tb, tb)), const((tb, tb)), const((tb, tb)), const((tb, tb)),
        ],
        out_specs=(
            pl.BlockSpec((step, GLA_V_WIDTH), lambda bi, i: (bi * nt + i, 0)),
            pl.BlockSpec((step, GLA_V_WIDTH), lambda bi, i: (bi * nt + nt - 1 - i, 0)),
        ),
        scratch_shapes=[
            pltpu.VMEM((GLA_HEADS, GLA_DK, GLA_DV), F32),
            pltpu.VMEM((GLA_HEADS, GLA_DK, GLA_DV), F32),
        ] + [pltpu.VMEM((GLA_HEADS, tb, GLA_DK), F32)] * (2 * GLA_SUB),
        compiler_params=_params(("parallel", "arbitrary")),
        name="gla",
    )(*([proj] * 8), w_f, b_f, w_b, b_b, tri_f.astype(BF16), tri_b.astype(BF16), tri_f, mask_b)


def _merge_kernel(att_ref, of_ref, ob_ref, g_ref, gla_ref, glb_ref, wa_ref, wg_ref, gn_ref,
                  bma_ref, bmb_ref, o_ref):
    blocks = [slice(c * MERGE_TN, (c + 1) * MERGE_TN) for c in range(D_MODEL // MERGE_TN)]
    att = att_ref[...]
    ga_aa = [_sigmoid(gla_ref[:, cs].astype(F32) + bma_ref[:, cs])
             * jnp.dot(att, wa_ref[:, cs], preferred_element_type=F32) for cs in blocks]
    o = of_ref[...] + ob_ref[...]
    gn = gn_ref[...]
    on = jnp.concatenate(
        [_rms(o[:, h * GLA_DV:(h + 1) * GLA_DV], gn) for h in range(GLA_HEADS)], axis=1)
    g = g_ref[...].astype(F32)
    gl = (on * (g * _sigmoid(g))).astype(BF16)
    for cs, a_part in zip(blocks, ga_aa):
        bb = jnp.dot(gl, wg_ref[:, cs], preferred_element_type=F32)
        gb = _sigmoid(glb_ref[:, cs].astype(F32) + bmb_ref[:, cs])
        o_ref[:, cs] = (a_part + gb * bb).astype(BF16)


def _merge(att, o_f, o_b, proj, w_attn, w_gla, gla_norm, b_merge):
    n = att.shape[0]
    tm = 512
    row = lambda w, c=0: pl.BlockSpec((tm, w), lambda i: (i, c))
    const = lambda r, w, c=0: pl.BlockSpec((r, w), lambda i: (0, c))
    resident = lambda r, w: pl.BlockSpec((r, w), lambda i: (0, 0), pipeline_mode=pl.Buffered(1))
    return pl.pallas_call(
        _merge_kernel,
        out_shape=jax.ShapeDtypeStruct((n, D_MODEL), BF16),
        grid=(n // tm,),
        in_specs=[
            row(ATTN_WIDTH), row(GLA_V_WIDTH), row(GLA_V_WIDTH),
            row(GLA_V_WIDTH, COL_GG // GLA_V_WIDTH),
            row(D_MODEL, 0), row(D_MODEL, 1),
            resident(ATTN_WIDTH, D_MODEL), resident(GLA_V_WIDTH, D_MODEL), const(1, GLA_DV),
            const(1, D_MODEL, 0), const(1, D_MODEL, 1),
        ],
        out_specs=row(D_MODEL),
        compiler_params=_params(("parallel",)),
        name="merge",
    )(att, o_f, o_b, proj, proj, proj, w_attn, w_gla, gla_norm, b_merge, b_merge)


def _outproj_kernel(x_ref, m_ref, w_ref, g_ref, h_ref, hn_ref):
    h = x_ref[...] + jnp.dot(m_ref[...], w_ref[...], preferred_element_type=F32)
    h_ref[...] = h
    hn_ref[...] = _rms(h, g_ref[...]).astype(BF16)


def _outproj(x, mixed, w_out, gain_mlp):
    n = x.shape[0]
    tm = 512
    row = pl.BlockSpec((tm, D_MODEL), lambda i: (i, 0))
    return pl.pallas_call(
        _outproj_kernel,
        out_shape=(jax.ShapeDtypeStruct((n, D_MODEL), F32),
                   jax.ShapeDtypeStruct((n, D_MODEL), BF16)),
        grid=(n // tm,),
        in_specs=[row, row, pl.BlockSpec((D_MODEL, D_MODEL), lambda i: (0, 0)),
                  pl.BlockSpec((1, D_MODEL), lambda i: (0, 0))],
        out_specs=(row, row),
        compiler_params=_params(("parallel",)),
        name="out_proj",
    )(x, mixed, w_out, gain_mlp)


def _mlp_kernel(h_ref, hn_ref, wu_ref, wd_ref, gf_ref, o_ref, *, final_norm):
    j = pl.program_id(1)
    last = pl.num_programs(1) - 1

    def down():
        u = jnp.maximum(jnp.dot(hn_ref[...], wu_ref[...], preferred_element_type=F32), 0.0)
        return jnp.dot((u * u).astype(BF16), wd_ref[...], preferred_element_type=F32)

    @pl.when(j == 0)
    def _():
        o_ref[...] = h_ref[...] + down()

    @pl.when(jnp.logical_and(j > 0, j < last))
    def _():
        o_ref[...] += down()

    @pl.when(j == last)
    def _():
        o = o_ref[...] + down()
        o_ref[...] = _rms(o, gf_ref[...]) if final_norm else o


def _mlp(h, hn, w_up, w_down, gain_final, final_norm):
    n = h.shape[0]
    tm = 512
    tf = 2048
    assert D_FF // tf >= 2
    row = pl.BlockSpec((tm, D_MODEL), lambda i, j: (i, 0))
    vec = pl.BlockSpec((1, D_MODEL), lambda i, j: (0, 0))
    return pl.pallas_call(
        functools.partial(_mlp_kernel, final_norm=final_norm),
        out_shape=jax.ShapeDtypeStruct((n, D_MODEL), F32),
        grid=(n // tm, D_FF // tf),
        in_specs=[
            row, row,
            pl.BlockSpec((D_MODEL, tf), lambda i, j: (0, j)),
            pl.BlockSpec((tf, D_MODEL), lambda i, j: (j, 0)),
            vec,
        ],
        out_specs=row,
        compiler_params=_params(("parallel", "arbitrary"), BIG_VMEM_LIMIT),
        name="mlp",
    )(h, hn, w_up, w_down, gain_final)


_SRC_WIDTHS = (ATTN_WIDTH, KV_WIDTH, KV_WIDTH, GLA_K_WIDTH, GLA_K_WIDTH, GLA_V_WIDTH, GLA_V_WIDTH,
               GLA_GATE_RANK, GLA_GATE_RANK, N_BRANCHES * D_MODEL)
_SRC_OFF = tuple(sum(_SRC_WIDTHS[:i]) for i in range(len(_SRC_WIDTHS) + 1))
D_IN_PROJ = _SRC_OFF[-1]
_PROJ_MOVES = ((COL_QA, 0), (COL_KVA, 1), (COL_KVA + KV_WIDTH, 2), (COL_QG, 3), (COL_KG, 4),
               (COL_VG, 5), (COL_GG, 6))


def _wproj_kernel(wt_ref, o_ref):
    def put(dst, src, width):
        o_ref[:, dst:dst + width] = wt_ref[src:src + width, :].T.astype(BF16)

    put(COL_GATE, _SRC_OFF[9], N_BRANCHES * D_MODEL)
    for dst, piece in _PROJ_MOVES:
        put(dst, _SRC_OFF[piece], _SRC_WIDTHS[piece])
    lr = wt_ref[_SRC_OFF[7]:_SRC_OFF[7] + LANES, :].T
    lane = lax.broadcasted_iota(jnp.int32, lr.shape, 1)
    o_ref[:, COL_LR:COL_LR + LANES] = jnp.where(lane < 2 * GLA_GATE_RANK, lr, 0.0).astype(BF16)
    o_ref[:, COL_LR + LANES:] = jnp.zeros((o_ref.shape[0], PROJ_W - COL_LR - LANES), BF16)


def _wproj(w_in):
    wt = jnp.swapaxes(w_in, 0, 1)
    return pl.pallas_call(
        _wproj_kernel,
        out_shape=jax.ShapeDtypeStruct((D_MODEL, PROJ_W), BF16),
        grid=(D_MODEL // LANES,),
        in_specs=[pl.BlockSpec((D_IN_PROJ, LANES), lambda i: (0, i))],
        out_specs=pl.BlockSpec((LANES, PROJ_W), lambda i: (i, 0)),
        compiler_params=_params(("parallel",)),
        name="w_proj_layout",
    )(wt)


def _prep_layer(w_in, w_gate_up_fwd, b_gate_fwd, w_gate_up_bwd, b_gate_bwd, w_attn_proj,
                w_gla_proj, w_out, w_up, w_down):
    w_proj = _wproj(w_in)
    zf = jnp.zeros((LANES - GLA_GATE_RANK, GLA_K_WIDTH), F32)
    zb = jnp.zeros((LANES - 2 * GLA_GATE_RANK, GLA_K_WIDTH), F32)
    w_f = jnp.concatenate([w_gate_up_fwd, zf], axis=0).astype(BF16)
    w_b = jnp.concatenate([jnp.zeros((GLA_GATE_RANK, GLA_K_WIDTH), F32), w_gate_up_bwd, zb],
                          axis=0).astype(BF16)
    return dict(
        w_proj=w_proj, w_f=w_f, w_b=w_b,
        b_f=b_gate_fwd.reshape(1, -1), b_b=b_gate_bwd.reshape(1, -1),
        w_attn=w_attn_proj.astype(BF16), w_gla=w_gla_proj.astype(BF16),
        w_out=w_out.astype(BF16), w_up=w_up.astype(BF16), w_down=w_down.astype(BF16))


def _layer(x, b, t, lw, norm_mix, q_norm, k_norm, gla_norm, b_merge, norm_mlp, norm_final,
           final_norm):
    proj = _inproj(x, norm_mix.reshape(1, -1), lw["w_proj"])
    q, k, vt = _prep(proj, _rope_tables(t), q_norm.reshape(1, -1), k_norm.reshape(1, -1), b, t)
    att = _attention(q, k, vt, _scores_bounded(q_norm, k_norm), b, t)
    o_f, o_b = _gla(proj, lw["w_f"], lw["b_f"], lw["w_b"], lw["b_b"], b, t)
    mixed = _merge(att, o_f, o_b, proj, lw["w_attn"], lw["w_gla"], gla_norm.reshape(1, -1),
                   b_merge.reshape(1, -1))
    h, hn = _outproj(x, mixed, lw["w_out"], norm_mlp.reshape(1, -1))
    return _mlp(h, hn, lw["w_up"], lw["w_down"], norm_final.reshape(1, -1), final_norm)


def kernel(x_prompt, x_sample, norm_mix, w_in, q_norm, k_norm, w_gate_up_fwd, b_gate_fwd,
           w_gate_up_bwd, b_gate_bwd, gla_norm, w_attn_proj, w_gla_proj, b_merge, w_out,
           norm_mlp, w_up, w_down, norm_final):
    layers = [
        _prep_layer(w_in[l], w_gate_up_fwd[l], b_gate_fwd[l], w_gate_up_bwd[l], b_gate_bwd[l],
                    w_attn_proj[l], w_gla_proj[l], w_out[l], w_up[l], w_down[l])
        for l in range(DEPTH)]

    def trunk(x):
        b, t, d = x.shape
        y = x.reshape(b * t, d)
        for l in range(DEPTH):
            y = _layer(y, b, t, layers[l], norm_mix[l], q_norm[l], k_norm[l], gla_norm[l],
                       b_merge[l], norm_mlp[l], norm_final, final_norm=(l == DEPTH - 1))
        return y.reshape(b, t, d)

    return trunk(x_prompt), trunk(x_sample)
```

```python
import functools
import itertools

import jax
import jax.numpy as jnp
from jax import lax
from jax.experimental import pallas as pl
from jax.experimental.pallas import tpu as pltpu

F32 = jnp.float32
BF16 = jnp.bfloat16

D_MODEL = 2048
DEPTH = 1
GRID_W = 64
HEAD_DIM = 128
N_Q_HEADS = 8
N_KV_HEADS = 2
Q_GROUP = N_Q_HEADS // N_KV_HEADS
ATTN_WIDTH = N_Q_HEADS * HEAD_DIM
KV_WIDTH = N_KV_HEADS * HEAD_DIM
ROPE_THETA = 10000.0
GLA_HEADS = 4
GLA_DK = 128
GLA_DV = 256
GLA_K_WIDTH = GLA_HEADS * GLA_DK
GLA_V_WIDTH = GLA_HEADS * GLA_DV
GLA_GATE_RANK = 16
GLA_GATE_NORMALIZER = 16.0
GLA_CHUNK = 64
N_BRANCHES = 2
D_FF = 4 * D_MODEL
NORM_EPS = 1e-6

LANES = 128
SUBLANES = 8
VMEM_LIMIT = 48 * 1024 * 1024
BIG_VMEM_LIMIT = 58 * 1024 * 1024

COL_GATE = 0
COL_QA = COL_GATE + N_BRANCHES * D_MODEL
COL_VG = COL_QA + ATTN_WIDTH
COL_GG = COL_VG + GLA_V_WIDTH
COL_KVA = COL_GG + GLA_V_WIDTH
COL_QG = COL_KVA + 2 * KV_WIDTH
COL_KG = COL_QG + GLA_K_WIDTH
COL_LR = COL_KG + GLA_K_WIDTH
PROJ_TN = 1792
PROJ_W = 5 * PROJ_TN

GLA_TB = 256
GLA_SUB = 2
MERGE_TN = D_MODEL // 2
ATT_TQ = 256
ATT_TK = 2048
ATT_KV_GROUP = 4
LOG2_E = 1.4426950408889634
ATT_BOUNDED_MAX_LOG2 = 60.0


def _params(sem, vmem_limit=VMEM_LIMIT):
    return pltpu.CompilerParams(dimension_semantics=sem, vmem_limit_bytes=vmem_limit)


def _sigmoid(x):
    return 1.0 / (1.0 + jnp.exp(-x))


def _rms(x, gain):
    ms = jnp.mean(x * x, axis=-1, keepdims=True)
    return x * lax.rsqrt(ms + NORM_EPS) * gain


def _inproj_kernel(x_ref, g_ref, w_ref, o_ref, xn_ref):
    @pl.when(pl.program_id(1) == 0)
    def _():
        xn_ref[...] = _rms(x_ref[...], g_ref[...]).astype(BF16)

    o_ref[...] = jnp.dot(xn_ref[...], w_ref[...], preferred_element_type=F32).astype(o_ref.dtype)


def _inproj(x, gain, w):
    n = x.shape[0]
    tm = 1024
    return pl.pallas_call(
        _inproj_kernel,
        out_shape=jax.ShapeDtypeStruct((n, PROJ_W), BF16),
        grid=(n // tm, PROJ_W // PROJ_TN),
        in_specs=[
            pl.BlockSpec((tm, D_MODEL), lambda i, j: (i, 0)),
            pl.BlockSpec((1, D_MODEL), lambda i, j: (0, 0)),
            pl.BlockSpec((D_MODEL, PROJ_TN), lambda i, j: (0, j)),
        ],
        out_specs=pl.BlockSpec((tm, PROJ_TN), lambda i, j: (i, j)),
        scratch_shapes=[pltpu.VMEM((tm, D_MODEL), BF16)],
        compiler_params=_params(("parallel", "arbitrary"), BIG_VMEM_LIMIT),
        name="in_proj",
    )(x, gain, w)


def _prep_kernel(q_ref, kv_ref, rt_ref, ct_ref, qg_ref, kg_ref, qo_ref, ko_ref, vt_ref):
    rows_per_tile = rt_ref.shape[1]

    def table(c):
        rt = rt_ref[c]
        by_row = jnp.concatenate(
            [jnp.broadcast_to(rt[g:g + 1], (GRID_W, HEAD_DIM)) for g in range(rows_per_tile)],
            axis=0)
        return by_row + jnp.tile(ct_ref[c], (rows_per_tile, 1))

    cos, sa, sb = table(0), table(1), table(2)

    def norm_rope(x, gain):
        y = _rms(x.astype(F32), gain)
        return y * cos + pltpu.roll(y, 96, 1) * sa + pltpu.roll(y, 32, 1) * sb

    qg = qg_ref[...]
    kg = kg_ref[...]
    for h in range(N_Q_HEADS):
        sl = slice(h * HEAD_DIM, (h + 1) * HEAD_DIM)
        qo_ref[:, sl] = (norm_rope(q_ref[:, sl], qg) * (LOG2_E * HEAD_DIM ** -0.5)).astype(BF16)
    for h in range(N_KV_HEADS):
        sl = slice(h * HEAD_DIM, (h + 1) * HEAD_DIM)
        ko_ref[:, sl] = norm_rope(kv_ref[:, sl], kg).astype(BF16)
        v = kv_ref[:, KV_WIDTH + h * HEAD_DIM:KV_WIDTH + (h + 1) * HEAD_DIM]
        vt_ref[0, h, 0] = v.astype(F32).T.astype(BF16)


def _prep(proj, tables, q_gain, k_gain, b, t):
    n = b * t
    tm = ATT_TK
    nt = t // tm
    row_tab, col_tab = tables
    rows_per_tile = tm // GRID_W
    row_spec = pl.BlockSpec((3, rows_per_tile, HEAD_DIM), lambda i: (0, i % nt, 0))
    col_spec = pl.BlockSpec((3, GRID_W, HEAD_DIM), lambda i: (0, 0, 0))
    gain_spec = pl.BlockSpec((1, HEAD_DIM), lambda i: (0, 0))
    return pl.pallas_call(
        _prep_kernel,
        out_shape=(
            jax.ShapeDtypeStruct((n, ATTN_WIDTH), BF16),
            jax.ShapeDtypeStruct((n, KV_WIDTH), BF16),
            jax.ShapeDtypeStruct((b, N_KV_HEADS, nt, HEAD_DIM, tm), BF16),
        ),
        grid=(n // tm,),
        in_specs=[
            pl.BlockSpec((tm, ATTN_WIDTH), lambda i: (i, COL_QA // ATTN_WIDTH)),
            pl.BlockSpec((tm, 2 * KV_WIDTH), lambda i: (i, COL_KVA // (2 * KV_WIDTH))),
            row_spec, col_spec, gain_spec, gain_spec,
        ],
        out_specs=(
            pl.BlockSpec((tm, ATTN_WIDTH), lambda i: (i, 0)),
            pl.BlockSpec((tm, KV_WIDTH), lambda i: (i, 0)),
            pl.BlockSpec((1, N_KV_HEADS, 1, HEAD_DIM, tm), lambda i: (i // nt, 0, i % nt, 0, 0)),
        ),
        compiler_params=_params(("parallel",)),
        name="qkv_prep",
    )(proj, proj, row_tab, col_tab, q_gain, k_gain)


def _rope_tables(t):
    n_rows = t // GRID_W
    sec = HEAD_DIM // 2
    inv_freq = ROPE_THETA ** (-jnp.arange(0, sec, 2, dtype=F32) / sec)
    def planes(ang):
        c, s, z = jnp.cos(ang), jnp.sin(ang), jnp.zeros_like(ang)
        return jnp.stack([jnp.concatenate([c, c], 1), jnp.concatenate([-s, z], 1),
                          jnp.concatenate([z, s], 1)])

    row = planes(jnp.arange(n_rows, dtype=F32)[:, None] * inv_freq[None, :])
    col = planes(jnp.arange(GRID_W, dtype=F32)[:, None] * inv_freq[None, :])
    row_tab = jnp.concatenate([row, jnp.zeros_like(row)], axis=2)
    col_tab = jnp.concatenate([jnp.zeros_like(col), col], axis=2)
    return row_tab, col_tab


def _attn_kernel(bounded_ref, q_ref, k_ref, vt_ref, o_ref, acc_ref, l_ref, m_ref, *, nk):
    tq = q_ref.shape[0]
    tk = vt_ref.shape[-1]
    cols = Q_GROUP * tq
    sub = l_ref.shape[0]

    qt = jnp.concatenate(
        [q_ref[:, g * HEAD_DIM:(g + 1) * HEAD_DIM].astype(F32).T.astype(BF16)
         for g in range(Q_GROUP)], axis=1)

    def kv_tile(j):
        return k_ref[pl.ds(pl.multiple_of(j * tk, tk), tk), :], vt_ref[0, 0, j]

    def sublane_partial_sum(p):
        return jnp.sum(p.reshape(tk // sub, sub, cols), axis=0)

    acc_ref[...] = jnp.zeros_like(acc_ref)
    l_ref[...] = jnp.zeros_like(l_ref)

    @pl.when(bounded_ref[0] != 0)
    def _():
        group = min(ATT_KV_GROUP, nk)

        def body(jg, carry):
            l_sum = pv_sum = None
            for u in range(group):
                k, vt = kv_tile(jg * group + u)
                p = jnp.exp2(jnp.dot(k, qt, preferred_element_type=F32))
                l_u = sublane_partial_sum(p)
                pv_u = jnp.dot(vt, p.astype(BF16), preferred_element_type=F32)
                l_sum = l_u if l_sum is None else l_sum + l_u
                pv_sum = pv_u if pv_sum is None else pv_sum + pv_u
            l_ref[...] += l_sum
            acc_ref[...] += pv_sum
            return carry

        lax.fori_loop(0, nk // group, body, 0)

    @pl.when(bounded_ref[0] == 0)
    def _():
        m_ref[...] = jnp.full_like(m_ref, -jnp.inf)

        def body(j, carry):
            k, vt = kv_tile(j)
            s = jnp.dot(k, qt, preferred_element_type=F32)
            m_old = m_ref[...]
            m_new = jnp.maximum(m_old, jnp.max(s, axis=0, keepdims=True))
            alpha = jnp.exp2(m_old - m_new)
            p = jnp.exp2(s - m_new)
            m_ref[...] = m_new
            l_ref[...] = alpha * l_ref[...] + sublane_partial_sum(p)
            acc_ref[...] = alpha * acc_ref[...] + jnp.dot(vt, p.astype(BF16),
                                                          preferred_element_type=F32)
            return carry

        lax.fori_loop(0, nk, body, 0)

    out_t = acc_ref[...] * (1.0 / jnp.sum(l_ref[...], axis=0, keepdims=True))
    for g in range(Q_GROUP):
        o_ref[:, g * HEAD_DIM:(g + 1) * HEAD_DIM] = out_t[:, g * tq:(g + 1) * tq].T.astype(BF16)


def _scores_bounded(q_gain, k_gain):
    bound = (1.02 * LOG2_E * HEAD_DIM ** 0.5) * jnp.max(jnp.abs(q_gain)) * jnp.max(jnp.abs(k_gain))
    return (bound <= ATT_BOUNDED_MAX_LOG2).astype(jnp.int32).reshape(1)


def _attention(q, k, vt, bounded, b, t):
    n = b * t
    tq = ATT_TQ
    nq = t // tq
    nk = t // ATT_TK
    gw = Q_GROUP * HEAD_DIM
    return pl.pallas_call(
        functools.partial(_attn_kernel, nk=nk),
        out_shape=jax.ShapeDtypeStruct((n, ATTN_WIDTH), BF16),
        grid=(b, N_KV_HEADS, nq),
        in_specs=[
            pl.BlockSpec(memory_space=pltpu.SMEM),
            pl.BlockSpec((tq, gw), lambda bi, h, i: (bi * nq + i, h)),
            pl.BlockSpec((t, HEAD_DIM), lambda bi, h, i: (bi, h)),
            pl.BlockSpec((1, 1, nk, HEAD_DIM, ATT_TK), lambda bi, h, i: (bi, h, 0, 0, 0)),
        ],
        out_specs=pl.BlockSpec((tq, gw), lambda bi, h, i: (bi * nq + i, h)),
        scratch_shapes=[
            pltpu.VMEM((HEAD_DIM, Q_GROUP * tq), F32),
            pltpu.VMEM((SUBLANES, Q_GROUP * tq), F32),
            pltpu.VMEM((1, Q_GROUP * tq), F32),
        ],
        compiler_params=_params(("parallel", "parallel", "arbitrary")),
        name="attention",
    )(bounded, q, k, vt)


def _gla_dir(q_ref, k_ref, v_ref, lr_ref, w_ref, b_ref, tri_ref, mask_ref, o_ref, s_ref, cs_ref,
             reverse):
    c_sz = GLA_CHUNK
    n_chunks = GLA_TB // c_sz
    z = jnp.dot(lr_ref[...], w_ref[...], preferred_element_type=F32) + b_ref[...]
    yield
    softplus2 = jnp.log2(1.0 + jnp.exp2(jnp.abs(z) * (-LOG2_E)))
    la = (jnp.minimum(z, 0.0) * (LOG2_E / GLA_GATE_NORMALIZER)
          - softplus2 * (1.0 / GLA_GATE_NORMALIZER))
    hi = la.astype(BF16)
    lo = (la - hi.astype(F32)).astype(BF16)
    tri = tri_ref[...]
    cs = (jnp.dot(tri, hi, preferred_element_type=F32)
          + jnp.dot(tri, lo, preferred_element_type=F32))
    yield
    for h in range(GLA_HEADS):
        cs_ref[h] = cs[:, h * GLA_DK:(h + 1) * GLA_DK]
    half = c_sz // 2
    edge_start = 0 if reverse else half - 1
    edge_row = [2 * c if reverse else 2 * c + 1 for c in range(n_chunks)]

    qe = (q_ref[...].astype(F32) * (GLA_DK ** -0.5)) * jnp.exp2(cs)
    ke = k_ref[...].astype(F32) * jnp.exp2(-cs)
    qe_b = qe.astype(BF16)
    ke_b = ke.astype(BF16)
    v_b = v_ref[...]
    mask = mask_ref[...] > 0.5
    chunk_of_lane = lax.broadcasted_iota(jnp.int32, (GLA_DK, GLA_TB), 1) // c_sz
    order = range(n_chunks - 1, -1, -1) if reverse else range(n_chunks)

    heads = range(GLA_HEADS)
    ks = [slice(h * GLA_DK, (h + 1) * GLA_DK) for h in heads]
    vs = [slice(h * GLA_DV, (h + 1) * GLA_DV) for h in heads]
    yield

    a = [jnp.where(mask, lax.dot_general(qe_b[:, ks[h]], ke_b[:, ks[h]], (((1,), (1,)), ((), ())),
                                         preferred_element_type=F32), 0.0).astype(BF16)
         for h in heads]
    yield

    decay_t, prod = [], []

    def recurrence(h):
        s = s_ref[h]
        for c in order:
            rs = slice(c * c_sz, (c + 1) * c_sz)
            r = edge_row[c]
            o_inter = jnp.dot(qe_b[rs, ks[h]], s.astype(BF16), preferred_element_type=F32)
            o_ref[rs, vs[h]] = prod[h][rs] + o_inter
            inc = prod[h][GLA_TB + c * GLA_DK:GLA_TB + (c + 1) * GLA_DK]
            s = decay_t[h][:, r:r + 1] * s + inc
        s_ref[h] = s

    for h in heads:
        edge = jnp.exp2(cs_ref[h, pl.ds(edge_start, 2 * n_chunks, stride=half), :])
        decay_t.append(jnp.tile(edge, (GLA_DK // (2 * n_chunks), 1)).T)
        kd = ke[:, ks[h]] * jnp.concatenate(
            [jnp.broadcast_to(edge[r:r + 1], (c_sz, GLA_DK)) for r in edge_row], axis=0)
        kd_t = kd.T
        lhs = jnp.concatenate(
            [a[h]]
            + [jnp.where(chunk_of_lane == c, kd_t, 0.0).astype(BF16) for c in range(n_chunks)],
            axis=0)
        prod.append(jnp.dot(lhs, v_b[:, vs[h]], preferred_element_type=F32))
        yield

    for h in heads:
        recurrence(h)
        yield


def _gla_kernel(qf_ref, kf_ref, vf_ref, lrf_ref, qb_ref, kb_ref, vb_ref, lrb_ref,
                wf_ref, bf_ref, wb_ref, bb_ref, trif_ref, trib_ref, maskf_ref, maskb_ref,
                of_ref, ob_ref, sf_ref, sb_ref, *cs_refs):
    @pl.when(pl.program_id(1) == 0)
    def _():
        sf_ref[...] = jnp.zeros_like(sf_ref)
        sb_ref[...] = jnp.zeros_like(sb_ref)

    def rows(ref, sub):
        return ref.at[pl.ds(sub * GLA_TB, GLA_TB)]

    def fwd(sub, cs_ref):
        return _gla_dir(rows(qf_ref, sub), rows(kf_ref, sub), rows(vf_ref, sub), rows(lrf_ref, sub),
                        wf_ref, bf_ref, trif_ref, maskf_ref, rows(of_ref, sub), sf_ref, cs_ref,
                        reverse=False)

    def bwd(sub, cs_ref):
        return _gla_dir(rows(qb_ref, sub), rows(kb_ref, sub), rows(vb_ref, sub), rows(lrb_ref, sub),
                        wb_ref, bb_ref, trib_ref, maskb_ref, rows(ob_ref, sub), sb_ref, cs_ref,
                        reverse=True)

    streams = []
    for s in range(GLA_SUB):
        streams.append(fwd(s, cs_refs[2 * s]))
        streams.append(bwd(GLA_SUB - 1 - s, cs_refs[2 * s + 1]))
    for _ in itertools.zip_longest(*streams):
        pass


def _gla_consts():
    i = jnp.arange(GLA_TB)[:, None]
    j = jnp.arange(GLA_TB)[None, :]
    same = (i // GLA_CHUNK) == (j // GLA_CHUNK)
    tri_f = (same & (j <= i)).astype(F32)
    tri_b = (same & (j >= i)).astype(F32)
    mask_b = (same & (j > i)).astype(F32)
    return tri_f, tri_b, mask_b


def _gla(proj, w_f, b_f, w_b, b_b, b, t):
    n = b * t
    tb = GLA_TB
    step = GLA_SUB * tb
    nt = t // step
    tri_f, tri_b, mask_b = _gla_consts()

    def fwd(width, col):
        return pl.BlockSpec((step, width), lambda bi, i: (bi * nt + i, col // width))

    def bwd(width, col):
        return pl.BlockSpec((step, width), lambda bi, i: (bi * nt + nt - 1 - i, col // width))

    def const(shape):
        return pl.BlockSpec(shape, lambda bi, i: (0, 0))

    pieces = ((GLA_K_WIDTH, COL_QG), (GLA_K_WIDTH, COL_KG), (GLA_V_WIDTH, COL_VG), (LANES, COL_LR))
    out_sd = jax.ShapeDtypeStruct((n, GLA_V_WIDTH), F32)
    return pl.pallas_call(
        _gla_kernel,
        out_shape=(out_sd, out_sd),
        grid=(b, nt),
        in_specs=[fwd(*p) for p in pieces] + [bwd(*p) for p in pieces] + [
            const((LANES, GLA_K_WIDTH)), const((1, GLA_K_WIDTH)),
            const((LANES, GLA_K_WIDTH)), const((1, GLA_K_WIDTH)),
            const((tb, tb)), const((tb, tb)), const((tb, tb)), const((tb, tb)),
        ],
        out_specs=(
            pl.BlockSpec((step, GLA_V_WIDTH), lambda bi, i: (bi * nt + i, 0)),
            pl.BlockSpec((step, GLA_V_WIDTH), lambda bi, i: (bi * nt + nt - 1 - i, 0)),
        ),
        scratch_shapes=[
            pltpu.VMEM((GLA_HEADS, GLA_DK, GLA_DV), F32),
            pltpu.VMEM((GLA_HEADS, GLA_DK, GLA_DV), F32),
        ] + [pltpu.VMEM((GLA_HEADS, tb, GLA_DK), F32)] * (2 * GLA_SUB),
        compiler_params=_params(("parallel", "arbitrary")),
        name="gla",
    )(*([proj] * 8), w_f, b_f, w_b, b_b, tri_f.astype(BF16), tri_b.astype(BF16), tri_f, mask_b)


def _merge_kernel(att_ref, of_ref, ob_ref, g_ref, gla_ref, glb_ref, wa_ref, wg_ref, gn_ref,
                  bma_ref, bmb_ref, o_ref):
    blocks = [slice(c * MERGE_TN, (c + 1) * MERGE_TN) for c in range(D_MODEL // MERGE_TN)]
    att = att_ref[...]
    ga_aa = [_sigmoid(gla_ref[:, cs].astype(F32) + bma_ref[:, cs])
             * jnp.dot(att, wa_ref[:, cs], preferred_element_type=F32) for cs in blocks]
    o = of_ref[...] + ob_ref[...]
    gn = gn_ref[...]
    on = jnp.concatenate(
        [_rms(o[:, h * GLA_DV:(h + 1) * GLA_DV], gn) for h in range(GLA_HEADS)], axis=1)
    g = g_ref[...].astype(F32)
    gl = (on * (g * _sigmoid(g))).astype(BF16)
    for cs, a_part in zip(blocks, ga_aa):
        bb = jnp.dot(gl, wg_ref[:, cs], preferred_element_type=F32)
        gb = _sigmoid(glb_ref[:, cs].astype(F32) + bmb_ref[:, cs])
        o_ref[:, cs] = (a_part + gb * bb).astype(BF16)


def _merge(att, o_f, o_b, proj, w_attn, w_gla, gla_norm, b_merge):
    n = att.shape[0]
    tm = 512
    row = lambda w, c=0: pl.BlockSpec((tm, w), lambda i: (i, c))
    const = lambda r, w, c=0: pl.BlockSpec((r, w), lambda i: (0, c))
    resident = lambda r, w: pl.BlockSpec((r, w), lambda i: (0, 0), pipeline_mode=pl.Buffered(1))
    return pl.pallas_call(
        _merge_kernel,
        out_shape=jax.ShapeDtypeStruct((n, D_MODEL), BF16),
        grid=(n // tm,),
        in_specs=[
            row(ATTN_WIDTH), row(GLA_V_WIDTH), row(GLA_V_WIDTH),
            row(GLA_V_WIDTH, COL_GG // GLA_V_WIDTH),
            row(D_MODEL, 0), row(D_MODEL, 1),
            resident(ATTN_WIDTH, D_MODEL), resident(GLA_V_WIDTH, D_MODEL), const(1, GLA_DV),
            const(1, D_MODEL, 0), const(1, D_MODEL, 1),
        ],
        out_specs=row(D_MODEL),
        compiler_params=_params(("parallel",)),
        name="merge",
    )(att, o_f, o_b, proj, proj, proj, w_attn, w_gla, gla_norm, b_merge, b_merge)


def _outproj_kernel(x_ref, m_ref, w_ref, g_ref, h_ref, hn_ref):
    h = x_ref[...] + jnp.dot(m_ref[...], w_ref[...], preferred_element_type=F32)
    h_ref[...] = h
    hn_ref[...] = _rms(h, g_ref[...]).astype(BF16)


def _outproj(x, mixed, w_out, gain_mlp):
    n = x.shape[0]
    tm = 512
    row = pl.BlockSpec((tm, D_MODEL), lambda i: (i, 0))
    return pl.pallas_call(
        _outproj_kernel,
        out_shape=(jax.ShapeDtypeStruct((n, D_MODEL), F32),
                   jax.ShapeDtypeStruct((n, D_MODEL), BF16)),
        grid=(n // tm,),
        in_specs=[row, row, pl.BlockSpec((D_MODEL, D_MODEL), lambda i: (0, 0)),
                  pl.BlockSpec((1, D_MODEL), lambda i: (0, 0))],
        out_specs=(row, row),
        compiler_params=_params(("parallel",)),
        name="out_proj",
    )(x, mixed, w_out, gain_mlp)


def _mlp_kernel(h_ref, hn_ref, wu_ref, wd_ref, gf_ref, o_ref, *, final_norm):
    j = pl.program_id(1)
    last = pl.num_programs(1) - 1

    def down():
        u = jnp.maximum(jnp.dot(hn_ref[...], wu_ref[...], preferred_element_type=F32), 0.0)
        return jnp.dot((u * u).astype(BF16), wd_ref[...], preferred_element_type=F32)

    @pl.when(j == 0)
    def _():
        o_ref[...] = h_ref[...] + down()

    @pl.when(jnp.logical_and(j > 0, j < last))
    def _():
        o_ref[...] += down()

    @pl.when(j == last)
    def _():
        o = o_ref[...] + down()
        o_ref[...] = _rms(o, gf_ref[...]) if final_norm else o


def _mlp(h, hn, w_up, w_down, gain_final, final_norm):
    n = h.shape[0]
    tm = 512
    tf = 2048
    assert D_FF // tf >= 2
    row = pl.BlockSpec((tm, D_MODEL), lambda i, j: (i, 0))
    vec = pl.BlockSpec((1, D_MODEL), lambda i, j: (0, 0))
    return pl.pallas_call(
        functools.partial(_mlp_kernel, final_norm=final_norm),
        out_shape=jax.ShapeDtypeStruct((n, D_MODEL), F32),
        grid=(n // tm, D_FF // tf),
        in_specs=[
            row, row,
            pl.BlockSpec((D_MODEL, tf), lambda i, j: (0, j)),
            pl.BlockSpec((tf, D_MODEL), lambda i, j: (j, 0)),
            vec,
        ],
        out_specs=row,
        compiler_params=_params(("parallel", "arbitrary"), BIG_VMEM_LIMIT),
        name="mlp",
    )(h, hn, w_up, w_down, gain_final)


_SRC_WIDTHS = (ATTN_WIDTH, KV_WIDTH, KV_WIDTH, GLA_K_WIDTH, GLA_K_WIDTH, GLA_V_WIDTH, GLA_V_WIDTH,
               GLA_GATE_RANK, GLA_GATE_RANK, N_BRANCHES * D_MODEL)
_SRC_OFF = tuple(sum(_SRC_WIDTHS[:i]) for i in range(len(_SRC_WIDTHS) + 1))
D_IN_PROJ = _SRC_OFF[-1]
_PROJ_MOVES = ((COL_QA, 0), (COL_KVA, 1), (COL_KVA + KV_WIDTH, 2), (COL_QG, 3), (COL_KG, 4),
               (COL_VG, 5), (COL_GG, 6))


def _wproj_kernel(wt_ref, o_ref):
    def put(dst, src, width):
        o_ref[:, dst:dst + width] = wt_ref[src:src + width, :].T.astype(BF16)

    put(COL_GATE, _SRC_OFF[9], N_BRANCHES * D_MODEL)
    for dst, piece in _PROJ_MOVES:
        put(dst, _SRC_OFF[piece], _SRC_WIDTHS[piece])
    lr = wt_ref[_SRC_OFF[7]:_SRC_OFF[7] + LANES, :].T
    lane = lax.broadcasted_iota(jnp.int32, lr.shape, 1)
    o_ref[:, COL_LR:COL_LR + LANES] = jnp.where(lane < 2 * GLA_GATE_RANK, lr, 0.0).astype(BF16)
    o_ref[:, COL_LR + LANES:] = jnp.zeros((o_ref.shape[0], PROJ_W - COL_LR - LANES), BF16)


def _wproj(w_in):
    wt = jnp.swapaxes(w_in, 0, 1)
    return pl.pallas_call(
        _wproj_kernel,
        out_shape=jax.ShapeDtypeStruct((D_MODEL, PROJ_W), BF16),
        grid=(D_MODEL // LANES,),
        in_specs=[pl.BlockSpec((D_IN_PROJ, LANES), lambda i: (0, i))],
        out_specs=pl.BlockSpec((LANES, PROJ_W), lambda i: (i, 0)),
        compiler_params=_params(("parallel",)),
        name="w_proj_layout",
    )(wt)


def _prep_layer(w_in, w_gate_up_fwd, b_gate_fwd, w_gate_up_bwd, b_gate_bwd, w_attn_proj,
                w_gla_proj, w_out, w_up, w_down):
    w_proj = _wproj(w_in)
    zf = jnp.zeros((LANES - GLA_GATE_RANK, GLA_K_WIDTH), F32)
    zb = jnp.zeros((LANES - 2 * GLA_GATE_RANK, GLA_K_WIDTH), F32)
    w_f = jnp.concatenate([w_gate_up_fwd, zf], axis=0).astype(BF16)
    w_b = jnp.concatenate([jnp.zeros((GLA_GATE_RANK, GLA_K_WIDTH), F32), w_gate_up_bwd, zb],
                          axis=0).astype(BF16)
    return dict(
        w_proj=w_proj, w_f=w_f, w_b=w_b,
        b_f=b_gate_fwd.reshape(1, -1), b_b=b_gate_bwd.reshape(1, -1),
        w_attn=w_attn_proj.astype(BF16), w_gla=w_gla_proj.astype(BF16),
        w_out=w_out.astype(BF16), w_up=w_up.astype(BF16), w_down=w_down.astype(BF16))


def _layer(x, b, t, lw, norm_mix, q_norm, k_norm, gla_norm, b_merge, norm_mlp, norm_final,
           final_norm):
    proj = _inproj(x, norm_mix.reshape(1, -1), lw["w_proj"])
    q, k, vt = _prep(proj, _rope_tables(t), q_norm.reshape(1, -1), k_norm.reshape(1, -1), b, t)
    att = _attention(q, k, vt, _scores_bounded(q_norm, k_norm), b, t)
    o_f, o_b = _gla(proj, lw["w_f"], lw["b_f"], lw["w_b"], lw["b_b"], b, t)
    mixed = _merge(att, o_f, o_b, proj, lw["w_attn"], lw["w_gla"], gla_norm.reshape(1, -1),
                   b_merge.reshape(1, -1))
    h, hn = _outproj(x, mixed, lw["w_out"], norm_mlp.reshape(1, -1))
    return _mlp(h, hn, lw["w_up"], lw["w_down"], norm_final.reshape(1, -1), final_norm)


def kernel(x_prompt, x_sample, norm_mix, w_in, q_norm, k_norm, w_gate_up_fwd, b_gate_fwd,
           w_gate_up_bwd, b_gate_bwd, gla_norm, w_attn_proj, w_gla_proj, b_merge, w_out,
           norm_mlp, w_up, w_down, norm_final):
    layers = [
        _prep_layer(w_in[l], w_gate_up_fwd[l], b_gate_fwd[l], w_gate_up_bwd[l], b_gate_bwd[l],
                    w_attn_proj[l], w_gla_proj[l], w_out[l], w_up[l], w_down[l])
        for l in range(DEPTH)]

    def trunk(x):
        b, t, d = x.shape
        y = x.reshape(b * t, d)
        for l in range(DEPTH):
            y = _layer(y, b, t, layers[l], norm_mix[l], q_norm[l], k_norm[l], gla_norm[l],
                       b_merge[l], norm_mlp[l], norm_final, final_norm=(l == DEPTH - 1))
        return y.reshape(b, t, d)

    return trunk(x_prompt), trunk(x_sample)
```
